```python
import math
import jax, jax.numpy as jnp
from jax import lax
import numpy as np

D_MODEL = 1024
BATCH = 8
SEQ = 8192
DEPTH = 2
DEC_BATCH = 16
DEC_SEQ = 16
PAST_LEN = 1024

CHUNK = 64
N_A_LAYERS = DEPTH // 2
N_B_LAYERS = DEPTH - N_A_LAYERS
HG_EXPAND = 128
HG_HEADS = D_MODEL // HG_EXPAND
HG_K = HG_EXPAND
HG_V = D_MODEL // HG_HEADS
DA_HEADS = 8
DA_HEAD_DIM = D_MODEL // (2 * DA_HEADS)
DA_V_DIM = 2 * DA_HEAD_DIM
D_FF = 4 * D_MODEL
Q_BLOCK = 128
NORM_EPS = 1e-6
LAMBDA_STD = 0.1

kernel_name = 'yoco_hgrn2_diffattn_stream_step'


def _rmsnorm(x, g):
    x32 = x.astype(jnp.float32)
    y = x32 * lax.rsqrt(jnp.mean(jnp.square(x32), axis=-1, keepdims=True) + NORM_EPS)
    return (y * g.astype(jnp.float32)).astype(x.dtype)


def _sqrelu_mlp(x, w_up, w_down):
    return jnp.square(jax.nn.relu(x @ w_up)) @ w_down


def _gla_chunk_step(S, xs):
    q, k, v, g = xs
    C = q.shape[1]
    b = jnp.cumsum(g, axis=1)
    causal = jnp.tril(jnp.ones((C, C), dtype=bool))
    diff = b[:, :, None] - b[:, None, :]
    decay = jnp.exp(jnp.where(causal[None, :, :, None, None], diff, -jnp.inf))
    a = jnp.einsum('bthk,bshk,btshk->bhts', q, k, decay)
    o = jnp.einsum('bhts,bshv->bthv', a, v) + jnp.einsum('bthk,bhkv->bthv', q * jnp.exp(b), S)
    b_last = b[:, -1]
    S_new = jnp.exp(b_last)[..., None] * S + jnp.einsum(
        'bshk,bshv->bhkv', k * jnp.exp(b_last[:, None] - b), v)
    return S_new, o


def _hgrn2(a, s0, w_in, lb, onorm_g, w_o, is_prompt):
    B, T, _ = a.shape
    q, f, i, g = jnp.split(a @ w_in, 4, axis=-1)
    fg = lb + (1.0 - lb) * jax.nn.sigmoid(f.astype(jnp.float32))
    shp = (B, T, HG_HEADS, HG_K)
    qh = jax.nn.silu(q.astype(jnp.float32)).reshape(shp)
    kh = (1.0 - fg).reshape(shp)
    gh = jnp.log(fg).reshape(shp)
    vh = i.astype(jnp.float32).reshape(B, T, HG_HEADS, HG_V)
    S0 = s0.astype(jnp.float32)
    if is_prompt:
        nc = T // CHUNK
        def to_chunks(z):
            return z.reshape((B, nc, CHUNK) + z.shape[2:]).swapaxes(0, 1)
        S_new, o = lax.scan(_gla_chunk_step, S0, (to_chunks(qh), to_chunks(kh), to_chunks(vh), to_chunks(gh)))
        o = o.swapaxes(0, 1).reshape(B, T, HG_HEADS, HG_V)
    else:
        S_new, o = _gla_chunk_step(S0, (qh, kh, vh, gh))
    gate = jax.nn.silu(g.astype(jnp.float32)).reshape(B, T, HG_HEADS, HG_V)
    o = _rmsnorm(o, onorm_g.reshape(HG_HEADS, HG_V)) * gate
    return o.reshape(B, T, D_MODEL).astype(a.dtype) @ w_o, S_new


def _chunk_mask(qpos, kpos):
    return (kpos[None, :] // CHUNK) <= (qpos[:, None] // CHUNK)


def _diff_core(q, k, v, mask, lam):
    s = jnp.einsum('bqhcd,bkhcd->bhcqk', q, k).astype(jnp.float32) * (DA_HEAD_DIM ** -0.5)
    s = jnp.where(mask, s, -jnp.inf)
    p = jax.nn.softmax(s, axis=-1)
    w = p[:, :, 0] - lam * p[:, :, 1]
    return jnp.einsum('bhqk,bkhe->bqhe', w.astype(v.dtype), v)


def _diff_attn(a, k_new, v_new, past_k, past_v, w_q, lam_p, subln_g, w_o, layer_idx):
    B, T, _ = a.shape
    q = (a @ w_q).reshape(B, T, DA_HEADS, 2, DA_HEAD_DIM)
    lam_init = 0.8 - 0.6 * math.exp(-0.3 * layer_idx)
    lp = lam_p.astype(jnp.float32)
    lam = jnp.exp(jnp.sum(lp[0] * lp[1])) - jnp.exp(jnp.sum(lp[2] * lp[3])) + lam_init
    if past_k is None:
        nb = T // Q_BLOCK
        qb = q.reshape(B, nb, Q_BLOCK, DA_HEADS, 2, DA_HEAD_DIM).swapaxes(0, 1)
        kpos = jnp.arange(T)
        def blk(args):
            qi, start = args
            qpos = start + jnp.arange(Q_BLOCK)
            return _diff_core(qi, k_new, v_new, _chunk_mask(qpos, kpos), lam)
        o = lax.map(blk, (qb, jnp.arange(nb) * Q_BLOCK))
        o = o.swapaxes(0, 1).reshape(B, T, DA_HEADS, DA_V_DIM)
    else:
        P = past_k.shape[1]
        k = jnp.concatenate([past_k.astype(k_new.dtype), k_new], axis=1)
        v = jnp.concatenate([past_v.astype(v_new.dtype), v_new], axis=1)
        mask = _chunk_mask(P + jnp.arange(T), jnp.arange(P + T))
        o = _diff_core(q, k, v, mask, lam)
    o = _rmsnorm(o, subln_g) * (1.0 - lam_init)
    return o.reshape(B, T, D_MODEL) @ w_o


def _trunk(x, hg_state0, past_k, past_v, norm_g, w_hgrn_in, hgrn_lb_logits, hgrn_onorm_g,
           w_hgrn_out, kv_norm_g, w_kv, w_dq, diff_lambda, diff_subln_g, w_do, w_up, w_down):
    is_prompt = past_k is None
    B, T, _ = x.shape
    lb_all = jnp.cumsum(jax.nn.softmax(hgrn_lb_logits.astype(jnp.float32), axis=0), axis=0)
    h = x
    new_states = []
    k_new = None
    v_new = None
    for l in range(DEPTH):
        a = _rmsnorm(h, norm_g[l, 0])
        if l < N_A_LAYERS:
            m, s_new = _hgrn2(a, hg_state0[l], w_hgrn_in[l], lb_all[l], hgrn_onorm_g[l],
                              w_hgrn_out[l], is_prompt)
            new_states.append(s_new.astype(hg_state0.dtype))
        else:
            j = l - N_A_LAYERS
            if j == 0:
                kv = _rmsnorm(h, kv_norm_g) @ w_kv
                k_new = kv[..., :D_MODEL].reshape(B, T, DA_HEADS, 2, DA_HEAD_DIM)
                v_new = kv[..., D_MODEL:].reshape(B, T, DA_HEADS, DA_V_DIM)
            m = _diff_attn(a, k_new, v_new, past_k, past_v, w_dq[j], diff_lambda[j],
                           diff_subln_g[j], w_do[j], l)
        h = h + _rmsnorm(m, norm_g[l, 1])
        f = _sqrelu_mlp(_rmsnorm(h, norm_g[l, 2]), w_up[l], w_down[l])
        h = h + _rmsnorm(f, norm_g[l, 3])
    return h, k_new, v_new, jnp.stack(new_states)


def setup_inputs(seed: int = 0) -> dict:
    key = jax.random.key(seed)
    ks = jax.random.split(key, 20)
    f32 = jnp.float32
    def w(k, shape, fan_in):
        return jax.random.normal(k, shape, f32) * (fan_in ** -0.5)
    def gain(k, shape):
        return 1.0 + 0.01 * jax.random.normal(k, shape, f32)
    return {
        'x_prompt': jax.random.normal(ks[0], (BATCH, SEQ, D_MODEL), f32),
        'x_sample': jax.random.normal(ks[1], (DEC_BATCH, DEC_SEQ, D_MODEL), f32),
        'cache_k': jax.random.normal(ks[2], (DEC_BATCH, PAST_LEN, DA_HEADS, 2, DA_HEAD_DIM), f32),
        'cache_v': jax.random.normal(ks[3], (DEC_BATCH, PAST_LEN, DA_HEADS, DA_V_DIM), f32),
        'state_hgrn': 0.5 * jax.random.normal(ks[4], (N_A_LAYERS, DEC_BATCH, HG_HEADS, HG_K, HG_V), f32),
        'norm_g': gain(ks[5], (DEPTH, 4, D_MODEL)),
        'w_hgrn_in': w(ks[6], (N_A_LAYERS, D_MODEL, 4 * D_MODEL), D_MODEL),
        'hgrn_lb_logits': 0.1 * jax.random.normal(ks[7], (N_A_LAYERS + 1, D_MODEL), f32),
        'hgrn_onorm_g': gain(ks[8], (N_A_LAYERS, D_MODEL)),
        'w_hgrn_out': w(ks[9], (N_A_LAYERS, D_MODEL, D_MODEL), D_MODEL),
        'kv_norm_g': gain(ks[10], (D_MODEL,)),
        'w_kv': w(ks[11], (D_MODEL, 2 * D_MODEL), D_MODEL),
        'w_dq': w(ks[12], (N_B_LAYERS, D_MODEL, D_MODEL), D_MODEL),
        'diff_lambda': LAMBDA_STD * jax.random.normal(ks[13], (N_B_LAYERS, 4, DA_HEAD_DIM), f32),
        'diff_subln_g': gain(ks[14], (N_B_LAYERS, DA_V_DIM)),
        'w_do': w(ks[15], (N_B_LAYERS, D_MODEL, D_MODEL), D_MODEL),
        'w_up': w(ks[16], (DEPTH, D_MODEL, D_FF), D_MODEL),
        'w_down': w(ks[17], (DEPTH, D_FF, D_MODEL), D_FF),
    }


def reference(x_prompt, x_sample, cache_k, cache_v, state_hgrn, norm_g, w_hgrn_in, hgrn_lb_logits,
              hgrn_onorm_g, w_hgrn_out, kv_norm_g, w_kv, w_dq, diff_lambda, diff_subln_g, w_do,
              w_up, w_down):
    s0 = jnp.zeros((N_A_LAYERS, x_prompt.shape[0], HG_HEADS, HG_K, HG_V), x_prompt.dtype)
    y_prompt, k_prompt, v_prompt, st_prompt = _trunk(
        x_prompt, s0, None, None, norm_g, w_hgrn_in, hgrn_lb_logits, hgrn_onorm_g, w_hgrn_out,
        kv_norm_g, w_kv, w_dq, diff_lambda, diff_subln_g, w_do, w_up, w_down)
    y_sample, k_sample, v_sample, st_sample = _trunk(
        x_sample, state_hgrn, cache_k, cache_v, norm_g, w_hgrn_in, hgrn_lb_logits, hgrn_onorm_g,
        w_hgrn_out, kv_norm_g, w_kv, w_dq, diff_lambda, diff_subln_g, w_do, w_up, w_down)
    return (y_prompt, y_sample, k_prompt, v_prompt, st_prompt, k_sample, v_sample, st_sample)
```

```python
import functools
import math

import jax
import jax.numpy as jnp
from jax import lax
from jax.experimental import pallas as pl
from jax.experimental.pallas import tpu as pltpu

F32 = jnp.float32
BF16 = jnp.bfloat16

NORM_EPS = 1e-6
MASK_CHUNK = 64
HEAD_W = 128
DA_D = 64
HG_PROMPT_CHUNK = 64
HG_SUB = 32
EXP_CLAMP = 80.0
NEG_BIG = -1e30
D_FF_TILE = 512
VMEM_LIMIT = 56 * 1024 * 1024

NT_DIMS = (((1,), (1,)), ((), ()))
TN_DIMS = (((0,), (0,)), ((), ()))


def _rms(x, g):
    ms = jnp.mean(x * x, axis=-1, keepdims=True)
    return x * lax.rsqrt(ms + NORM_EPS) * g


def _sigmoid(x):
    return 1.0 / (1.0 + jnp.exp(-x))


def _const_spec(shape):
    nd = len(shape)
    return pl.BlockSpec(shape, lambda *_: (0,) * nd, pipeline_mode=pl.Buffered(1))


def _hgrn_kernel(h_ref, g_ref, w_ref, lbl_ref, og_ref, s0_ref, o_ref, sout_ref, st_ref,
                 *, layer, chunk, sub, heads):
    t = pl.program_id(1)
    nt = pl.num_programs(1)
    tm = h_ref.shape[1]
    d = h_ref.shape[2]

    @pl.when(t == 0)
    def _():
        for hd in range(heads):
            st_ref[hd] = s0_ref[0, hd].T

    lg = lbl_ref[...]
    lg = jnp.exp(lg - jnp.max(lg, axis=0, keepdims=True))
    lb = jnp.sum(lg[: layer + 1], axis=0, keepdims=True) / jnp.sum(lg, axis=0, keepdims=True)

    a = _rms(h_ref[0], g_ref[...]).astype(BF16)
    proj = jnp.dot(a, w_ref[...], preferred_element_type=F32)
    og = og_ref[...]

    r_i = lax.broadcasted_iota(jnp.int32, (chunk, chunk), 0)
    c_i = lax.broadcasted_iota(jnp.int32, (chunk, chunk), 1)
    tri = (c_i <= r_i).astype(BF16)

    nsub = chunk // sub
    for c in range(tm // chunk):
        rows = slice(c * chunk, (c + 1) * chunk)
        qr = proj[rows, 0:d]
        fr = proj[rows, d:2 * d]
        vr = proj[rows, 2 * d:3 * d]
        gr = proj[rows, 3 * d:4 * d]

        fg = lb + (1.0 - lb) * _sigmoid(fr)
        gl = jnp.log(fg)
        p0 = gl.astype(BF16)
        r1 = gl - p0.astype(F32)
        p1 = r1.astype(BF16)
        p2 = (r1 - p1.astype(F32)).astype(BF16)
        b = (jnp.dot(tri, p0, preferred_element_type=F32)
             + jnp.dot(tri, p1, preferred_element_type=F32)
             + jnp.dot(tri, p2, preferred_element_type=F32))

        qh = qr * _sigmoid(qr)
        kh = 1.0 - fg
        vb = vr.astype(BF16)
        b_last = b[chunk - 1:chunk, :]
        q_in = (qh * jnp.exp(b)).astype(BF16)
        k_st = (kh * jnp.exp(b_last - b)).astype(BF16)
        e_last = jnp.exp(b_last)

        q_sub, k_sub = [], []
        for j in range(nsub):
            ref = b[j * sub + sub // 2 - 1: j * sub + sub // 2, :]
            ncol = (j + 1) * sub
            q_sub.append((qh[j * sub:ncol] * jnp.exp(jnp.minimum(b[j * sub:ncol] - ref, EXP_CLAMP))
                          ).astype(BF16))
            k_sub.append((kh[:ncol] * jnp.exp(jnp.minimum(ref - b[:ncol], EXP_CLAMP))).astype(BF16))

        gate = gr * _sigmoid(gr)
        outs = []
        for hd in range(heads):
            sl = slice(hd * HEAD_W, (hd + 1) * HEAD_W)
            st = st_ref[hd]
            o_h = lax.dot_general(q_in[:, sl], st.astype(BF16), NT_DIMS,
                                  preferred_element_type=F32)
            parts = []
            for j in range(nsub):
                ncol = (j + 1) * sub
                sc = lax.dot_general(q_sub[j][:, sl], k_sub[j][:, sl], NT_DIMS,
                                     preferred_element_type=F32)
                rr = lax.broadcasted_iota(jnp.int32, (sub, ncol), 0) + j * sub
                cc = lax.broadcasted_iota(jnp.int32, (sub, ncol), 1)
                sc = jnp.where(cc <= rr, sc, 0.0)
                parts.append(jnp.dot(sc.astype(BF16), vb[:ncol, sl], preferred_element_type=F32))
            o_h = o_h + (parts[0] if nsub == 1 else jnp.concatenate(parts, axis=0))
            st_ref[hd] = st * e_last[:, sl] + lax.dot_general(
                vb[:, sl], k_st[:, sl], TN_DIMS, preferred_element_type=F32)
            o_h = _rms(o_h, og[:, sl]) * gate[:, sl]
            outs.append(o_h)
        o_ref[0, rows, :] = jnp.concatenate(outs, axis=1).astype(o_ref.dtype)

    @pl.when(t == nt - 1)
    def _():
        for hd in range(heads):
            sout_ref[0, hd] = st_ref[hd].T


def _hgrn_mixer(h, g, w_in, lb_logits, onorm_g, s0, *, layer, chunk, tm):
    bsz, seq, d = h.shape
    heads = d // HEAD_W
    sub = min(HG_SUB, chunk)
    kern = functools.partial(_hgrn_kernel, layer=layer, chunk=chunk, sub=sub, heads=heads)
    return pl.pallas_call(
        kern,
        grid=(bsz, seq // tm),
        in_specs=[
            pl.BlockSpec((1, tm, d), lambda b, t: (b, t, 0)),
            _const_spec((1, d)),
            _const_spec((d, 4 * d)),
            _const_spec(lb_logits.shape),
            _const_spec((1, d)),
            pl.BlockSpec((1, heads, HEAD_W, HEAD_W), lambda b, t: (b, 0, 0, 0)),
        ],
        out_specs=[
            pl.BlockSpec((1, tm, d), lambda b, t: (b, t, 0)),
            pl.BlockSpec((1, heads, HEAD_W, HEAD_W), lambda b, t: (b, 0, 0, 0)),
        ],
        out_shape=[
            jax.ShapeDtypeStruct((bsz, seq, d), BF16),
            jax.ShapeDtypeStruct((bsz, heads, HEAD_W, HEAD_W), F32),
        ],
        scratch_shapes=[pltpu.VMEM((heads, HEAD_W, HEAD_W), F32)],
        compiler_params=pltpu.CompilerParams(
            dimension_semantics=("parallel", "arbitrary"), vmem_limit_bytes=VMEM_LIMIT),
        name="hgrn_mixer",
    )(h, g.reshape(1, d), w_in, lb_logits, onorm_g.reshape(1, d), s0)


def _out_mlp_kernel(x_ref, h_ref, wo_ref, g1_ref, g2_ref, wu_ref, wd_ref, g3_ref, y_ref):
    m = jnp.dot(x_ref[...], wo_ref[...], preferred_element_type=F32)
    h1 = h_ref[...] + _rms(m, g1_ref[...])
    a2 = _rms(h1, g2_ref[...]).astype(BF16)
    d_ff = wu_ref.shape[1]
    acc = None
    for c in range(d_ff // D_FF_TILE):
        cs = slice(c * D_FF_TILE, (c + 1) * D_FF_TILE)
        u = jnp.dot(a2, wu_ref[:, cs], preferred_element_type=F32)
        u = jnp.maximum(u, 0.0)
        u = (u * u).astype(BF16)
        part = jnp.dot(u, wd_ref[cs, :], preferred_element_type=F32)
        acc = part if acc is None else acc + part
    y_ref[...] = h1 + _rms(acc, g3_ref[...])


def _out_mlp(x, h, w_o, g1, g2, w_up, w_down, g3, *, tm):
    n, d = h.shape
    d_ff = w_up.shape[1]
    row = lambda i: (i, 0)
    return pl.pallas_call(
        _out_mlp_kernel,
        grid=(n // tm,),
        in_specs=[
            pl.BlockSpec((tm, d), row),
            pl.BlockSpec((tm, d), row),
            _const_spec((d, d)),
            _const_spec((1, d)),
            _const_spec((1, d)),
            _const_spec((d, d_ff)),
            _const_spec((d_ff, d)),
            _const_spec((1, d)),
        ],
        out_specs=pl.BlockSpec((tm, d), row),
        out_shape=jax.ShapeDtypeStruct((n, d), F32),
        compiler_params=pltpu.CompilerParams(
            dimension_semantics=("parallel",), vmem_limit_bytes=VMEM_LIMIT),
        name="out_mlp",
    )(x, h, w_o, g1.reshape(1, d), g2.reshape(1, d), w_up, w_down, g3.reshape(1, d))


def _kvq_kernel(h_ref, gkv_ref, gq_ref, wkv_ref, wq_ref, k_ref, v_ref, kb_ref, vb_ref, qb_ref):
    h = h_ref[...]
    d = h.shape[1]
    akv = _rms(h, gkv_ref[...]).astype(BF16)
    kv = jnp.dot(akv, wkv_ref[...], preferred_element_type=F32)
    k = kv[:, :d]
    v = kv[:, d:]
    k_ref[...] = k
    v_ref[...] = v
    kb_ref[...] = k.astype(BF16)
    vb_ref[...] = v.astype(BF16)
    aq = _rms(h, gq_ref[...]).astype(BF16)
    q = jnp.dot(aq, wq_ref[...], preferred_element_type=F32)
    qb_ref[...] = (q * (DA_D ** -0.5)).astype(BF16)


def _kvq_proj(h, g_kv, g_q, w_kv, w_q, *, tm):
    n, d = h.shape
    row = lambda i: (i, 0)
    blk = pl.BlockSpec((tm, d), row)
    return pl.pallas_call(
        _kvq_kernel,
        grid=(n // tm,),
        in_specs=[blk, _const_spec((1, d)), _const_spec((1, d)),
                  _const_spec((d, 2 * d)), _const_spec((d, d))],
        out_specs=[blk, blk, blk, blk, blk],
        out_shape=[jax.ShapeDtypeStruct((n, d), F32), jax.ShapeDtypeStruct((n, d), F32),
                   jax.ShapeDtypeStruct((n, d), BF16), jax.ShapeDtypeStruct((n, d), BF16),
                   jax.ShapeDtypeStruct((n, d), BF16)],
        compiler_params=pltpu.CompilerParams(
            dimension_semantics=("parallel",), vmem_limit_bytes=VMEM_LIMIT),
        name="kvq_proj",
    )(h, g_kv.reshape(1, d), g_q.reshape(1, d), w_kv, w_q)


def _lambda_value(lp, lam_init):
    a = jnp.sum(lp[0:1] * lp[1:2], axis=-1, keepdims=True)
    b = jnp.sum(lp[2:3] * lp[3:4], axis=-1, keepdims=True)
    return jnp.exp(a) - jnp.exp(b) + lam_init


def _attn_kernel(lam_ref, q_ref, k_ref, v_ref, sg_ref, o_ref,
                 vt_ref, m_ref, l_ref, acc_ref, *, blk, lam_init):
    i = pl.program_id(2)
    seq = k_ref.shape[1]

    @pl.when(i == 0)
    def _():
        def tr(c, carry):
            vb = v_ref[0, pl.ds(pl.multiple_of(c * blk, blk), blk), :].astype(F32)
            vt_ref[c] = vb.T.astype(BF16)
            return carry
        lax.fori_loop(0, seq // blk, tr, 0)

    q = q_ref[0]
    lane = lax.broadcasted_iota(jnp.int32, q.shape, 1)
    zero = jnp.zeros_like(q)
    qc = (jnp.where(lane < DA_D, q, zero), jnp.where(lane >= DA_D, q, zero))

    m_ref[...] = jnp.full(m_ref.shape, NEG_BIG, F32)
    l_ref[...] = jnp.zeros(l_ref.shape, F32)
    acc_ref[...] = jnp.zeros(acc_ref.shape, F32)

    def step(j, masked):
        kb = k_ref[0, pl.ds(pl.multiple_of(j * blk, blk), blk), :]
        vt = vt_ref[j]
        if masked:
            kr = lax.broadcasted_iota(jnp.int32, (blk, blk), 0) // MASK_CHUNK
            qr = lax.broadcasted_iota(jnp.int32, (blk, blk), 1) // MASK_CHUNK
            vis = kr <= qr
        for c in range(2):
            s = lax.dot_general(kb, qc[c], NT_DIMS, preferred_element_type=F32)
            if masked:
                s = jnp.where(vis, s, NEG_BIG)
            m_old = m_ref[c:c + 1, :]
            m_new = jnp.maximum(m_old, jnp.max(s, axis=0, keepdims=True))
            alpha = jnp.exp(m_old - m_new)
            p = jnp.exp(s - m_new)
            l_ref[c:c + 1, :] = alpha * l_ref[c:c + 1, :] + jnp.sum(p, axis=0, keepdims=True)
            m_ref[c:c + 1, :] = m_new
            acc_ref[c] = acc_ref[c] * alpha + jnp.dot(vt, p.astype(BF16),
                                                       preferred_element_type=F32)

    def body(j, carry):
        step(j, False)
        return carry
    lax.fori_loop(0, i, body, 0)
    step(i, True)

    lam = _lambda_value(lam_ref[...], lam_init)
    o_t = acc_ref[0] / l_ref[0:1, :] - lam * (acc_ref[1] / l_ref[1:2, :])
    ms = jnp.mean(o_t * o_t, axis=0, keepdims=True)
    o_t = o_t * lax.rsqrt(ms + NORM_EPS)
    o = o_t.T * sg_ref[...] * (1.0 - lam_init)
    o_ref[0] = o.astype(o_ref.dtype)


def _attn_prompt(qb, kb, vb, lam_p, subln_g, *, lam_init, blk):
    bsz, seq, d = qb.shape
    heads = d // HEAD_W
    kern = functools.partial(_attn_kernel, blk=blk, lam_init=lam_init)
    qspec = pl.BlockSpec((1, blk, HEAD_W), lambda b, h, i: (b, i, h))
    kvspec = pl.BlockSpec((1, seq, HEAD_W), lambda b, h, i: (b, 0, h))
    return pl.pallas_call(
        kern,
        grid=(bsz, heads, seq // blk),
        in_specs=[pl.BlockSpec(lam_p.shape, lambda b, h, i: (0, 0)), qspec, kvspec, kvspec,
                  pl.BlockSpec((1, HEAD_W), lambda b, h, i: (0, 0))],
        out_specs=qspec,
        out_shape=jax.ShapeDtypeStruct((bsz, seq, d), BF16),
        scratch_shapes=[
            pltpu.VMEM((seq // blk, HEAD_W, blk), BF16),
            pltpu.VMEM((2, blk), F32),
            pltpu.VMEM((2, blk), F32),
            pltpu.VMEM((2, HEAD_W, blk), F32),
        ],
        compiler_params=pltpu.CompilerParams(
            dimension_semantics=("parallel", "parallel", "arbitrary"),
            vmem_limit_bytes=VMEM_LIMIT),
        name="diff_attn_prompt",
    )(lam_p, qb, kb, vb, subln_g.reshape(1, HEAD_W))


def _attn_cache_kernel(lam_ref, q_ref, kn_ref, vn_ref, pk_ref, pv_ref, sg_ref, o_ref,
                       *, lam_init, heads):
    tq = q_ref.shape[1]
    past = pk_ref.shape[1]
    lam = _lambda_value(lam_ref[...], lam_init)
    q = q_ref[0]
    lane = lax.broadcasted_iota(jnp.int32, (tq, HEAD_W), 1)
    qpos = (lax.broadcasted_iota(jnp.int32, (tq, past), 0) + past) // MASK_CHUNK
    vis_p = (lax.broadcasted_iota(jnp.int32, (tq, past), 1) // MASK_CHUNK) <= qpos
    qpos_n = (lax.broadcasted_iota(jnp.int32, (tq, tq), 0) + past) // MASK_CHUNK
    vis_n = ((lax.broadcasted_iota(jnp.int32, (tq, tq), 1) + past) // MASK_CHUNK) <= qpos_n
    outs = []
    for hd in range(heads):
        sl = slice(hd * HEAD_W, (hd + 1) * HEAD_W)
        qh = q[:, sl]
        zero = jnp.zeros_like(qh)
        kp = pk_ref[0, :, sl].astype(BF16)
        kn = kn_ref[0, :, sl]
        w_p, w_n = None, None
        for c in range(2):
            qc = jnp.where(lane < DA_D, qh, zero) if c == 0 else jnp.where(lane >= DA_D, qh, zero)
            sp = lax.dot_general(qc, kp, NT_DIMS, preferred_element_type=F32)
            sn = lax.dot_general(qc, kn, NT_DIMS, preferred_element_type=F32)
            sp = jnp.where(vis_p, sp, NEG_BIG)
            sn = jnp.where(vis_n, sn, NEG_BIG)
            m = jnp.maximum(jnp.max(sp, axis=-1, keepdims=True), jnp.max(sn, axis=-1, keepdims=True))
            pp = jnp.exp(sp - m)
            pn = jnp.exp(sn - m)
            den = jnp.sum(pp, axis=-1, keepdims=True) + jnp.sum(pn, axis=-1, keepdims=True)
            pp = pp / den
            pn = pn / den
            if c == 0:
                w_p, w_n = pp, pn
            else:
                w_p, w_n = w_p - lam * pp, w_n - lam * pn
        o_h = (jnp.dot(w_p.astype(BF16), pv_ref[0, :, sl].astype(BF16), preferred_element_type=F32)
               + jnp.dot(w_n.astype(BF16), vn_ref[0, :, sl], preferred_element_type=F32))
        outs.append(_rms(o_h, sg_ref[...]) * (1.0 - lam_init))
    o_ref[0] = jnp.concatenate(outs, axis=1).astype(o_ref.dtype)


def _attn_cache(qb, kb, vb, past_k, past_v, lam_p, subln_g, *, lam_init):
    bsz, tq, d = qb.shape
    past = past_k.shape[1]
    heads = d // HEAD_W
    kern = functools.partial(_attn_cache_kernel, lam_init=lam_init, heads=heads)
    new = pl.BlockSpec((1, tq, d), lambda b: (b, 0, 0))
    old = pl.BlockSpec((1, past, d), lambda b: (b, 0, 0))
    return pl.pallas_call(
        kern,
        grid=(bsz,),
        in_specs=[pl.BlockSpec(lam_p.shape, lambda b: (0, 0)), new, new, new, old, old,
                  pl.BlockSpec((1, HEAD_W), lambda b: (0, 0))],
        out_specs=new,
        out_shape=jax.ShapeDtypeStruct((bsz, tq, d), BF16),
        compiler_params=pltpu.CompilerParams(
            dimension_semantics=("parallel",), vmem_limit_bytes=VMEM_LIMIT),
        name="diff_attn_cache",
    )(lam_p, qb, kb, vb, past_k, past_v, subln_g.reshape(1, HEAD_W))


def _trunk(x, hg_state0, past_k, past_v, p, *, hg_chunk, hg_tm, row_tm, attn_blk):
    bsz, seq, d = x.shape
    n = bsz * seq
    depth = p["norm_g"].shape[0]
    n_a = p["w_hgrn_in"].shape[0]
    heads = d // HEAD_W
    h = x
    states = []
    k32 = v32 = kb = vb = None
    for l in range(depth):
        g = p["norm_g"][l]
        if l < n_a:
            o, s_new = _hgrn_mixer(h, g[0], p["w_hgrn_in"][l], p["hgrn_lb_logits"],
                                   p["hgrn_onorm_g"][l], hg_state0[l],
                                   layer=l, chunk=hg_chunk, tm=hg_tm)
            states.append(s_new)
            w_o = p["w_hgrn_out"][l]
        else:
            j = l - n_a
            if j == 0:
                k32, v32, kb, vb, qb = _kvq_proj(h.reshape(n, d), p["kv_norm_g"], g[0],
                                                 p["w_kv"], p["w_dq"][j], tm=row_tm)
                kb = kb.reshape(bsz, seq, d)
                vb = vb.reshape(bsz, seq, d)
                qb = qb.reshape(bsz, seq, d)
            else:
                raise NotImplementedError("one attention layer per shared K/V is supported")
            lam_init = 0.8 - 0.6 * math.exp(-0.3 * l)
            if past_k is None:
                o = _attn_prompt(qb, kb, vb, p["diff_lambda"][j], p["diff_subln_g"][j],
                                 lam_init=lam_init, blk=attn_blk)
            else:
                o = _attn_cache(qb, kb, vb, past_k.reshape(bsz, -1, d), past_v.reshape(bsz, -1, d),
                                p["diff_lambda"][j], p["diff_subln_g"][j], lam_init=lam_init)
            w_o = p["w_do"][j]
        h = _out_mlp(o.reshape(n, d), h.reshape(n, d), w_o, g[1], g[2],
                     p["w_up"][l], p["w_down"][l], g[3], tm=row_tm).reshape(bsz, seq, d)
    k_new = k32.reshape(bsz, seq, heads, 2, DA_D)
    v_new = v32.reshape(bsz, seq, heads, HEAD_W)
    return h, k_new, v_new, jnp.stack(states)


def kernel(x_prompt, x_sample, cache_k, cache_v, state_hgrn, norm_g, w_hgrn_in, hgrn_lb_logits,
           hgrn_onorm_g, w_hgrn_out, kv_norm_g, w_kv, w_dq, diff_lambda, diff_subln_g, w_do,
           w_up, w_down):
    p = dict(
        norm_g=norm_g, hgrn_lb_logits=hgrn_lb_logits, hgrn_onorm_g=hgrn_onorm_g,
        kv_norm_g=kv_norm_g, diff_lambda=diff_lambda, diff_subln_g=diff_subln_g,
        w_hgrn_in=w_hgrn_in.astype(BF16), w_hgrn_out=w_hgrn_out.astype(BF16),
        w_kv=w_kv.astype(BF16), w_dq=w_dq.astype(BF16), w_do=w_do.astype(BF16),
        w_up=w_up.astype(BF16), w_down=w_down.astype(BF16),
    )
    n_a = w_hgrn_in.shape[0]
    bp, tp, d = x_prompt.shape
    bs, ts, _ = x_sample.shape
    heads = d // HEAD_W
    s0 = jnp.zeros((n_a, bp, heads, HEAD_W, HEAD_W), x_prompt.dtype)
    y_p, k_p, v_p, st_p = _trunk(x_prompt, s0, None, None, p, hg_chunk=HG_PROMPT_CHUNK,
                                 hg_tm=256, row_tm=512, attn_blk=256)
    y_s, k_s, v_s, st_s = _trunk(x_sample, state_hgrn, cache_k, cache_v, p, hg_chunk=ts,
                                 hg_tm=ts, row_tm=bs * ts, attn_blk=None)
    return (y_p, y_s, k_p, v_p, st_p, k_s, v_s, st_s)
```

```python
import functools
import math

import jax
import jax.numpy as jnp
from jax import lax
from jax.experimental import pallas as pl
from jax.experimental.pallas import tpu as pltpu

F32 = jnp.float32
BF16 = jnp.bfloat16

NORM_EPS = 1e-6
MASK_CHUNK = 64
HEAD_W = 128
DA_D = 64
HG_PROMPT_CHUNK = 64
HG_SUB = 32
EXP_CLAMP = 80.0
NEG_BIG = -1e30
D_FF_TILE = 512
VT_PAD = 16
LOG2_E = math.log2(math.e)
VMEM_LIMIT = 56 * 1024 * 1024

NT_DIMS = (((1,), (1,)), ((), ()))
TN_DIMS = (((0,), (0,)), ((), ()))


def _rms(x, g):
    ms = jnp.mean(x * x, axis=-1, keepdims=True)
    return x * lax.rsqrt(ms + NORM_EPS) * g


def _sigmoid(x):
    return 1.0 / (1.0 + jnp.exp(-x))


def _const_spec(shape):
    nd = len(shape)
    return pl.BlockSpec(shape, lambda *_: (0,) * nd, pipeline_mode=pl.Buffered(1))


def _hgrn_kernel(h_ref, g_ref, w_ref, lbl_ref, og_ref, s0_ref, o_ref, sout_ref, st_ref,
                 *, layer, chunk, sub, heads):
    t = pl.program_id(1)
    nt = pl.num_programs(1)
    tm = h_ref.shape[1]
    d = h_ref.shape[2]

    @pl.when(t == 0)
    def _():
        for hd in range(heads):
            st_ref[hd] = s0_ref[0, hd].T

    lg = lbl_ref[...]
    lg = jnp.exp(lg - jnp.max(lg, axis=0, keepdims=True))
    lb = jnp.sum(lg[: layer + 1], axis=0, keepdims=True) / jnp.sum(lg, axis=0, keepdims=True)

    a = _rms(h_ref[0], g_ref[...]).astype(BF16)
    proj = jnp.dot(a, w_ref[...], preferred_element_type=F32)
    og = og_ref[...]

    r_i = lax.broadcasted_iota(jnp.int32, (chunk, chunk), 0)
    c_i = lax.broadcasted_iota(jnp.int32, (chunk, chunk), 1)
    tri = (c_i <= r_i).astype(BF16)

    nsub = chunk // sub
    n_chunks = tm // chunk
    heads_sl = [slice(hd * HEAD_W, (hd + 1) * HEAD_W) for hd in range(heads)]

    def prep(c):
        rows = slice(c * chunk, (c + 1) * chunk)
        qr = proj[rows, 0:d]
        fr = proj[rows, d:2 * d]
        vr = proj[rows, 2 * d:3 * d]
        gr = proj[rows, 3 * d:4 * d]

        fg = lb + (1.0 - lb) * _sigmoid(fr)
        gl = jnp.log(fg)
        p0 = gl.astype(BF16)
        r1 = gl - p0.astype(F32)
        p1 = r1.astype(BF16)
        p2 = (r1 - p1.astype(F32)).astype(BF16)
        b = (jnp.dot(tri, p0, preferred_element_type=F32)
             + jnp.dot(tri, p1, preferred_element_type=F32)
             + jnp.dot(tri, p2, preferred_element_type=F32))

        qh = qr * _sigmoid(qr)
        kh = 1.0 - fg
        vb = vr.astype(BF16)
        b_last = b[chunk - 1:chunk, :]
        q_in = (qh * jnp.exp(b)).astype(BF16)
        k_st = (kh * jnp.exp(b_last - b)).astype(BF16)
        e_last = jnp.exp(b_last)

        q_sub, k_sub = [], []
        for j in range(nsub):
            ref = b[j * sub + sub // 2 - 1: j * sub + sub // 2, :]
            ncol = (j + 1) * sub
            q_sub.append((qh[j * sub:ncol] * jnp.exp(jnp.minimum(b[j * sub:ncol] - ref, EXP_CLAMP))
                          ).astype(BF16))
            k_sub.append((kh[:ncol] * jnp.exp(jnp.minimum(ref - b[:ncol], EXP_CLAMP))).astype(BF16))

        gate = gr * _sigmoid(gr)
        return dict(rows=rows, q_in=q_in, k_st=k_st, e_last=e_last, q_sub=q_sub, k_sub=k_sub,
                    vb=vb, gate=gate)

    def state_free_products(pc):
        scores, incs = [], []
        for sl in heads_sl:
            per_sub = []
            for j in range(nsub):
                ncol = (j + 1) * sub
                sc = lax.dot_general(pc["q_sub"][j][:, sl], pc["k_sub"][j][:, sl], NT_DIMS,
                                     preferred_element_type=F32)
                rr = lax.broadcasted_iota(jnp.int32, (sub, ncol), 0) + j * sub
                cc = lax.broadcasted_iota(jnp.int32, (sub, ncol), 1)
                per_sub.append(jnp.where(cc <= rr, sc, 0.0).astype(BF16))
            scores.append(per_sub)
            incs.append(lax.dot_general(pc["vb"][:, sl], pc["k_st"][:, sl], TN_DIMS,
                                        preferred_element_type=F32))
        return scores, incs

    def finish(pc, scores, incs):
        outs = []
        for hd, sl in enumerate(heads_sl):
            o_h = lax.dot_general(pc["q_in"][:, sl], st[hd].astype(BF16), NT_DIMS,
                                  preferred_element_type=F32)
            parts = [jnp.dot(scores[hd][j], pc["vb"][:(j + 1) * sub, sl],
                             preferred_element_type=F32) for j in range(nsub)]
            o_h = o_h + (parts[0] if nsub == 1 else jnp.concatenate(parts, axis=0))
            st[hd] = st[hd] * pc["e_last"][:, sl] + incs[hd]
            outs.append(_rms(o_h, og[:, sl]) * pc["gate"][:, sl])
        o_ref[0, pc["rows"], :] = jnp.concatenate(outs, axis=1).astype(o_ref.dtype)

    st = [st_ref[hd] for hd in range(heads)]
    prev = None
    for c in range(n_chunks):
        pc = prep(c)
        cur = (pc,) + state_free_products(pc)
        if prev is not None:
            finish(*prev)
        prev = cur
    finish(*prev)
    for hd in range(heads):
        st_ref[hd] = st[hd]

    @pl.when(t == nt - 1)
    def _():
        for hd in range(heads):
            sout_ref[0, hd] = st_ref[hd].T


def _hgrn_mixer(h, g, w_in, lb_logits, onorm_g, s0, *, layer, chunk, tm):
    bsz, seq, d = h.shape
    heads = d // HEAD_W
    sub = min(HG_SUB, chunk)
    kern = functools.partial(_hgrn_kernel, layer=layer, chunk=chunk, sub=sub, heads=heads)
    return pl.pallas_call(
        kern,
        grid=(bsz, seq // tm),
        in_specs=[
            pl.BlockSpec((1, tm, d), lambda b, t: (b, t, 0)),
            _const_spec((1, d)),
            _const_spec((d, 4 * d)),
            _const_spec(lb_logits.shape),
            _const_spec((1, d)),
            pl.BlockSpec((1, heads, HEAD_W, HEAD_W), lambda b, t: (b, 0, 0, 0)),
        ],
        out_specs=[
            pl.BlockSpec((1, tm, d), lambda b, t: (b, t, 0)),
            pl.BlockSpec((1, heads, HEAD_W, HEAD_W), lambda b, t: (b, 0, 0, 0)),
        ],
        out_shape=[
            jax.ShapeDtypeStruct((bsz, seq, d), BF16),
            jax.ShapeDtypeStruct((bsz, heads, HEAD_W, HEAD_W), F32),
        ],
        scratch_shapes=[pltpu.VMEM((heads, HEAD_W, HEAD_W), F32)],
        compiler_params=pltpu.CompilerParams(
            dimension_semantics=("parallel", "arbitrary"), vmem_limit_bytes=VMEM_LIMIT),
        name="hgrn_mixer",
    )(h, g.reshape(1, d), w_in, lb_logits, onorm_g.reshape(1, d), s0)


def _out_mlp_kernel(x_ref, h_ref, wo_ref, g1_ref, g2_ref, wu_ref, wd_ref, g3_ref, y_ref):
    m = jnp.dot(x_ref[...], wo_ref[...], preferred_element_type=F32)
    h1 = h_ref[...] + _rms(m, g1_ref[...])
    a2 = _rms(h1, g2_ref[...]).astype(BF16)
    d_ff = wu_ref.shape[1]
    acc = None
    for c in range(d_ff // D_FF_TILE):
        cs = slice(c * D_FF_TILE, (c + 1) * D_FF_TILE)
        u = jnp.dot(a2, wu_ref[:, cs], preferred_element_type=F32)
        u = jnp.maximum(u, 0.0)
        u = (u * u).astype(BF16)
        part = jnp.dot(u, wd_ref[cs, :], preferred_element_type=F32)
        acc = part if acc is None else acc + part
    y_ref[...] = h1 + _rms(acc, g3_ref[...])


def _out_mlp(x, h, w_o, g1, g2, w_up, w_down, g3, *, tm):
    n, d = h.shape
    d_ff = w_up.shape[1]
    row = lambda i: (i, 0)
    return pl.pallas_call(
        _out_mlp_kernel,
        grid=(n // tm,),
        in_specs=[
            pl.BlockSpec((tm, d), row),
            pl.BlockSpec((tm, d), row),
            _const_spec((d, d)),
            _const_spec((1, d)),
            _const_spec((1, d)),
            _const_spec((d, d_ff)),
            _const_spec((d_ff, d)),
            _const_spec((1, d)),
        ],
        out_specs=pl.BlockSpec((tm, d), row),
        out_shape=jax.ShapeDtypeStruct((n, d), F32),
        compiler_params=pltpu.CompilerParams(
            dimension_semantics=("parallel",), vmem_limit_bytes=VMEM_LIMIT),
        name="out_mlp",
    )(x, h, w_o, g1.reshape(1, d), g2.reshape(1, d), w_up, w_down, g3.reshape(1, d))


def _kvq_kernel(h_ref, gkv_ref, gq_ref, wkv_ref, wq_ref, k_ref, v_ref, kb_ref, vb_ref, qb_ref):
    h = h_ref[...]
    d = h.shape[1]
    akv = _rms(h, gkv_ref[...]).astype(BF16)
    kv = jnp.dot(akv, wkv_ref[...], preferred_element_type=F32)
    k = kv[:, :d]
    v = kv[:, d:]
    k_ref[...] = k
    v_ref[...] = v
    kb_ref[...] = k.astype(BF16)
    vb_ref[...] = v.astype(BF16)
    aq = _rms(h, gq_ref[...]).astype(BF16)
    q = jnp.dot(aq, wq_ref[...], preferred_element_type=F32)
    qb_ref[...] = (q * (DA_D ** -0.5 * LOG2_E)).astype(BF16)


def _kvq_proj(h, g_kv, g_q, w_kv, w_q, *, tm):
    n, d = h.shape
    row = lambda i: (i, 0)
    blk = pl.BlockSpec((tm, d), row)
    return pl.pallas_call(
        _kvq_kernel,
        grid=(n // tm,),
        in_specs=[blk, _const_spec((1, d)), _const_spec((1, d)),
                  _const_spec((d, 2 * d)), _const_spec((d, d))],
        out_specs=[blk, blk, blk, blk, blk],
        out_shape=[jax.ShapeDtypeStruct((n, d), F32), jax.ShapeDtypeStruct((n, d), F32),
                   jax.ShapeDtypeStruct((n, d), BF16), jax.ShapeDtypeStruct((n, d), BF16),
                   jax.ShapeDtypeStruct((n, d), BF16)],
        compiler_params=pltpu.CompilerParams(
            dimension_semantics=("parallel",), vmem_limit_bytes=VMEM_LIMIT),
        name="kvq_proj",
    )(h, g_kv.reshape(1, d), g_q.reshape(1, d), w_kv, w_q)


def _lambda_value(lp, lam_init):
    a = jnp.sum(lp[0:1] * lp[1:2], axis=-1, keepdims=True)
    b = jnp.sum(lp[2:3] * lp[3:4], axis=-1, keepdims=True)
    return jnp.exp(a) - jnp.exp(b) + lam_init


def _attn_kernel(lam_ref, q_ref, k_ref, v_ref, sg_ref, o_ref,
                 vt_ref, s_ref, m_ref, acc_ref, *, blk, lam_init):
    i = pl.program_id(2)
    seq = k_ref.shape[1]

    @pl.when(i == 0)
    def _():
        def tr(c, carry):
            vb = v_ref[0, pl.ds(pl.multiple_of(c * blk, blk), blk), :].astype(F32)
            vt_ref[c, 0:HEAD_W, :] = vb.T.astype(BF16)
            vt_ref[c, HEAD_W:, :] = jnp.ones((VT_PAD, blk), BF16)
            return carry
        lax.fori_loop(0, seq // blk, tr, 0)

    q = q_ref[0]
    lane = lax.broadcasted_iota(jnp.int32, q.shape, 1)
    zero = jnp.zeros_like(q)
    qc = (jnp.where(lane < DA_D, q, zero), jnp.where(lane >= DA_D, q, zero))

    m_ref[...] = jnp.full(m_ref.shape, NEG_BIG, F32)
    acc_ref[...] = jnp.zeros(acc_ref.shape, F32)

    def scores(j, slot, masked):
        kb = k_ref[0, pl.ds(pl.multiple_of(j * blk, blk), blk), :]
        if masked:
            kr = lax.broadcasted_iota(jnp.int32, (blk, blk), 0) // MASK_CHUNK
            qr = lax.broadcasted_iota(jnp.int32, (blk, blk), 1) // MASK_CHUNK
            vis = kr <= qr
        for c in range(2):
            s = lax.dot_general(kb, qc[c], NT_DIMS, preferred_element_type=F32)
            if masked:
                s = jnp.where(vis, s, NEG_BIG)
            s_ref[slot, c] = s

    def consume(j, slot):
        vt = vt_ref[j]
        for c in range(2):
            s = s_ref[slot, c]
            m_old = m_ref[c:c + 1, :]
            m_new = jnp.maximum(m_old, jnp.max(s, axis=0, keepdims=True))
            alpha = jnp.exp2(m_old - m_new)
            p = jnp.exp2(s - m_new).astype(BF16)
            m_ref[c:c + 1, :] = m_new
            acc_ref[c] = acc_ref[c] * alpha + jnp.dot(vt, p, preferred_element_type=F32)

    scores(i, 0, True)

    def body(j, carry):
        consume(jnp.where(j == 0, i, j - 1), j % 2)
        scores(j, (j + 1) % 2, False)
        return carry
    lax.fori_loop(0, i, body, 0)
    consume(jnp.maximum(i - 1, 0), i % 2)

    lam = _lambda_value(lam_ref[...], lam_init)
    o_t = (acc_ref[0, 0:HEAD_W, :] / acc_ref[0, HEAD_W:HEAD_W + 1, :]
           - lam * (acc_ref[1, 0:HEAD_W, :] / acc_ref[1, HEAD_W:HEAD_W + 1, :]))
    ms = jnp.mean(o_t * o_t, axis=0, keepdims=True)
    o_t = o_t * lax.rsqrt(ms + NORM_EPS)
    o = o_t.T * sg_ref[...] * (1.0 - lam_init)
    o_ref[0] = o.astype(o_ref.dtype)


def _attn_prompt(qb, kb, vb, lam_p, subln_g, *, lam_init, blk):
    bsz, seq, d = qb.shape
    heads = d // HEAD_W
    kern = functools.partial(_attn_kernel, blk=blk, lam_init=lam_init)
    qspec = pl.BlockSpec((1, blk, HEAD_W), lambda b, h, i: (b, i, h))
    kvspec = pl.BlockSpec((1, seq, HEAD_W), lambda b, h, i: (b, 0, h))
    return pl.pallas_call(
        kern,
        grid=(bsz, heads, seq // blk),
        in_specs=[pl.BlockSpec(lam_p.shape, lambda b, h, i: (0, 0)), qspec, kvspec, kvspec,
                  pl.BlockSpec((1, HEAD_W), lambda b, h, i: (0, 0))],
        out_specs=qspec,
        out_shape=jax.ShapeDtypeStruct((bsz, seq, d), BF16),
        scratch_shapes=[
            pltpu.VMEM((seq // blk, HEAD_W + VT_PAD, blk), BF16),
            pltpu.VMEM((2, 2, blk, blk), F32),
            pltpu.VMEM((2, blk), F32),
            pltpu.VMEM((2, HEAD_W + VT_PAD, blk), F32),
        ],
        compiler_params=pltpu.CompilerParams(
            dimension_semantics=("parallel", "parallel", "arbitrary"),
            vmem_limit_bytes=VMEM_LIMIT),
        name="diff_attn_prompt",
    )(lam_p, qb, kb, vb, subln_g.reshape(1, HEAD_W))


def _attn_cache_kernel(lam_ref, q_ref, kn_ref, vn_ref, pk_ref, pv_ref, sg_ref, o_ref,
                       *, lam_init, heads):
    tq = q_ref.shape[1]
    past = pk_ref.shape[1]
    lam = _lambda_value(lam_ref[...], lam_init)
    q = q_ref[0]
    lane = lax.broadcasted_iota(jnp.int32, (tq, HEAD_W), 1)
    qpos = (lax.broadcasted_iota(jnp.int32, (tq, past), 0) + past) // MASK_CHUNK
    vis_p = (lax.broadcasted_iota(jnp.int32, (tq, past), 1) // MASK_CHUNK) <= qpos
    qpos_n = (lax.broadcasted_iota(jnp.int32, (tq, tq), 0) + past) // MASK_CHUNK
    vis_n = ((lax.broadcasted_iota(jnp.int32, (tq, tq), 1) + past) // MASK_CHUNK) <= qpos_n
    outs = []
    for hd in range(heads):
        sl = slice(hd * HEAD_W, (hd + 1) * HEAD_W)
        qh = q[:, sl]
        zero = jnp.zeros_like(qh)
        kp = pk_ref[0, :, sl].astype(BF16)
        kn = kn_ref[0, :, sl]
        w_p, w_n = None, None
        for c in range(2):
            qc = jnp.where(lane < DA_D, qh, zero) if c == 0 else jnp.where(lane >= DA_D, qh, zero)
            sp = lax.dot_general(qc, kp, NT_DIMS, preferred_element_type=F32)
            sn = lax.dot_general(qc, kn, NT_DIMS, preferred_element_type=F32)
            sp = jnp.where(vis_p, sp, NEG_BIG)
            sn = jnp.where(vis_n, sn, NEG_BIG)
            m = jnp.maximum(jnp.max(sp, axis=-1, keepdims=True), jnp.max(sn, axis=-1, keepdims=True))
            pp = jnp.exp2(sp - m)
            pn = jnp.exp2(sn - m)
            den = jnp.sum(pp, axis=-1, keepdims=True) + jnp.sum(pn, axis=-1, keepdims=True)
            pp = pp / den
            pn = pn / den
            if c == 0:
                w_p, w_n = pp, pn
            else:
                w_p, w_n = w_p - lam * pp, w_n - lam * pn
        o_h = (jnp.dot(w_p.astype(BF16), pv_ref[0, :, sl].astype(BF16), preferred_element_type=F32)
               + jnp.dot(w_n.astype(BF16), vn_ref[0, :, sl], preferred_element_type=F32))
        outs.append(_rms(o_h, sg_ref[...]) * (1.0 - lam_init))
    o_ref[0] = jnp.concatenate(outs, axis=1).astype(o_ref.dtype)


def _attn_cache(qb, kb, vb, past_k, past_v, lam_p, subln_g, *, lam_init):
    bsz, tq, d = qb.shape
    past = past_k.shape[1]
    heads = d // HEAD_W
    kern = functools.partial(_attn_cache_kernel, lam_init=lam_init, heads=heads)
    new = pl.BlockSpec((1, tq, d), lambda b: (b, 0, 0))
    old = pl.BlockSpec((1, past, d), lambda b: (b, 0, 0))
    return pl.pallas_call(
        kern,
        grid=(bsz,),
        in_specs=[pl.BlockSpec(lam_p.shape, lambda b: (0, 0)), new, new, new, old, old,
                  pl.BlockSpec((1, HEAD_W), lambda b: (0, 0))],
        out_specs=new,
        out_shape=jax.ShapeDtypeStruct((bsz, tq, d), BF16),
        compiler_params=pltpu.CompilerParams(
            dimension_semantics=("parallel",), vmem_limit_bytes=VMEM_LIMIT),
        name="diff_attn_cache",
    )(lam_p, qb, kb, vb, past_k, past_v, subln_g.reshape(1, HEAD_W))


def _trunk(x, hg_state0, past_k, past_v, p, *, hg_chunk, hg_tm, row_tm, attn_blk):
    bsz, seq, d = x.shape
    n = bsz * seq
    depth = p["norm_g"].shape[0]
    n_a = p["w_hgrn_in"].shape[0]
    heads = d // HEAD_W
    h = x
    states = []
    k32 = v32 = kb = vb = None
    for l in range(depth):
        g = p["norm_g"][l]
        if l < n_a:
            o, s_new = _hgrn_mixer(h, g[0], p["w_hgrn_in"][l], p["hgrn_lb_logits"],
                                   p["hgrn_onorm_g"][l], hg_state0[l],
                                   layer=l, chunk=hg_chunk, tm=hg_tm)
            states.append(s_new)
            w_o = p["w_hgrn_out"][l]
        else:
            j = l - n_a
            if j == 0:
                k32, v32, kb, vb, qb = _kvq_proj(h.reshape(n, d), p["kv_norm_g"], g[0],
                                                 p["w_kv"], p["w_dq"][j], tm=row_tm)
                kb = kb.reshape(bsz, seq, d)
                vb = vb.reshape(bsz, seq, d)
                qb = qb.reshape(bsz, seq, d)
            else:
                raise NotImplementedError("one attention layer per shared K/V is supported")
            lam_init = 0.8 - 0.6 * math.exp(-0.3 * l)
            if past_k is None:
                o = _attn_prompt(qb, kb, vb, p["diff_lambda"][j], p["diff_subln_g"][j],
                                 lam_init=lam_init, blk=attn_blk)
            else:
                o = _attn_cache(qb, kb, vb, past_k.reshape(bsz, -1, d), past_v.reshape(bsz, -1, d),
                                p["diff_lambda"][j], p["diff_subln_g"][j], lam_init=lam_init)
            w_o = p["w_do"][j]
        h = _out_mlp(o.reshape(n, d), h.reshape(n, d), w_o, g[1], g[2],
                     p["w_up"][l], p["w_down"][l], g[3], tm=row_tm).reshape(bsz, seq, d)
    k_new = k32.reshape(bsz, seq, heads, 2, DA_D)
    v_new = v32.reshape(bsz, seq, heads, HEAD_W)
    return h, k_new, v_new, jnp.stack(states)


def kernel(x_prompt, x_sample, cache_k, cache_v, state_hgrn, norm_g, w_hgrn_in, hgrn_lb_logits,
           hgrn_onorm_g, w_hgrn_out, kv_norm_g, w_kv, w_dq, diff_lambda, diff_subln_g, w_do,
           w_up, w_down):
    p = dict(
        norm_g=norm_g, hgrn_lb_logits=hgrn_lb_logits, hgrn_onorm_g=hgrn_onorm_g,
        kv_norm_g=kv_norm_g, diff_lambda=diff_lambda, diff_subln_g=diff_subln_g,
        w_hgrn_in=w_hgrn_in.astype(BF16), w_hgrn_out=w_hgrn_out.astype(BF16),
        w_kv=w_kv.astype(BF16), w_dq=w_dq.astype(BF16), w_do=w_do.astype(BF16),
        w_up=w_up.astype(BF16), w_down=w_down.astype(BF16),
    )
    n_a = w_hgrn_in.shape[0]
    bp, tp, d = x_prompt.shape
    bs, ts, _ = x_sample.shape
    heads = d // HEAD_W
    s0 = jnp.zeros((n_a, bp, heads, HEAD_W, HEAD_W), x_prompt.dtype)
    y_p, k_p, v_p, st_p = _trunk(x_prompt, s0, None, None, p, hg_chunk=HG_PROMPT_CHUNK,
                                 hg_tm=256, row_tm=512, attn_blk=1024)
    y_s, k_s, v_s, st_s = _trunk(x_sample, state_hgrn, cache_k, cache_v, p, hg_chunk=ts,
                                 hg_tm=ts, row_tm=bs * ts, attn_blk=None)
    return (y_p, y_s, k_p, v_p, st_p, k_s, v_s, st_s)
```

```python
import functools
import math

import jax
import jax.numpy as jnp
from jax import lax
from jax.experimental import pallas as pl
from jax.experimental.pallas import tpu as pltpu

F32 = jnp.float32
BF16 = jnp.bfloat16

NORM_EPS = 1e-6
MASK_CHUNK = 64
HEAD_W = 128
DA_D = 64
HG_PROMPT_CHUNK = 64
HG_SUB = 32
EXP_CLAMP = 80.0
NEG_BIG = -1e30
D_FF_TILE = 512
VT_PAD = 16
LOG2_E = math.log2(math.e)
VMEM_LIMIT = 56 * 1024 * 1024

NT_DIMS = (((1,), (1,)), ((), ()))
TN_DIMS = (((0,), (0,)), ((), ()))


def _rms(x, g):
    ms = jnp.mean(x * x, axis=-1, keepdims=True)
    return x * lax.rsqrt(ms + NORM_EPS) * g


def _sigmoid(x):
    return 1.0 / (1.0 + jnp.exp(-x))


def _const_spec(shape):
    nd = len(shape)
    return pl.BlockSpec(shape, lambda *_: (0,) * nd, pipeline_mode=pl.Buffered(1))


def _hgrn_kernel(h_ref, g_ref, w_ref, lbl_ref, og_ref, s0_ref, o_ref, sout_ref, st_ref,
                 *, layer, chunk, sub, heads):
    t = pl.program_id(1)
    nt = pl.num_programs(1)
    tm = h_ref.shape[1]
    d = h_ref.shape[2]

    @pl.when(t == 0)
    def _():
        for hd in range(heads):
            st_ref[hd] = s0_ref[0, hd].T

    lg = lbl_ref[...]
    lg = jnp.exp(lg - jnp.max(lg, axis=0, keepdims=True))
    lb = jnp.sum(lg[: layer + 1], axis=0, keepdims=True) / jnp.sum(lg, axis=0, keepdims=True)

    a = _rms(h_ref[0], g_ref[...]).astype(BF16)
    proj = jnp.dot(a, w_ref[...], preferred_element_type=F32)
    og = og_ref[...]

    r_i = lax.broadcasted_iota(jnp.int32, (chunk, chunk), 0)
    c_i = lax.broadcasted_iota(jnp.int32, (chunk, chunk), 1)
    tri = (c_i <= r_i).astype(BF16)

    nsub = chunk // sub
    n_chunks = tm // chunk
    heads_sl = [slice(hd * HEAD_W, (hd + 1) * HEAD_W) for hd in range(heads)]

    def prep(c):
        rows = slice(c * chunk, (c + 1) * chunk)
        qr = proj[rows, 0:d]
        fr = proj[rows, d:2 * d]
        vr = proj[rows, 2 * d:3 * d]
        gr = proj[rows, 3 * d:4 * d]

        fg = lb + (1.0 - lb) * _sigmoid(fr)
        gl = jnp.log(fg)
        p0 = gl.astype(BF16)
        r1 = gl - p0.astype(F32)
        p1 = r1.astype(BF16)
        p2 = (r1 - p1.astype(F32)).astype(BF16)
        b = (jnp.dot(tri, p0, preferred_element_type=F32)
             + jnp.dot(tri, p1, preferred_element_type=F32)
             + jnp.dot(tri, p2, preferred_element_type=F32))

        qh = qr * _sigmoid(qr)
        kh = 1.0 - fg
        vb = vr.astype(BF16)
        b_last = b[chunk - 1:chunk, :]
        q_in = (qh * jnp.exp(b)).astype(BF16)
        k_st = (kh * jnp.exp(b_last - b)).astype(BF16)
        e_last = jnp.exp(b_last)

        q_sub, k_sub = [], []
        for j in range(nsub):
            ref = b[j * sub + sub // 2 - 1: j * sub + sub // 2, :]
            ncol = (j + 1) * sub
            q_sub.append((qh[j * sub:ncol] * jnp.exp(jnp.minimum(b[j * sub:ncol] - ref, EXP_CLAMP))
                          ).astype(BF16))
            k_sub.append((kh[:ncol] * jnp.exp(jnp.minimum(ref - b[:ncol], EXP_CLAMP))).astype(BF16))

        gate = gr * _sigmoid(gr)
        return dict(rows=rows, q_in=q_in, k_st=k_st, e_last=e_last, q_sub=q_sub, k_sub=k_sub,
                    vb=vb, gate=gate)

    def state_free_products(pc):
        scores, incs = [], []
        for sl in heads_sl:
            per_sub = []
            for j in range(nsub):
                ncol = (j + 1) * sub
                sc = lax.dot_general(pc["q_sub"][j][:, sl], pc["k_sub"][j][:, sl], NT_DIMS,
                                     preferred_element_type=F32)
                rr = lax.broadcasted_iota(jnp.int32, (sub, ncol), 0) + j * sub
                cc = lax.broadcasted_iota(jnp.int32, (sub, ncol), 1)
                per_sub.append(jnp.where(cc <= rr, sc, 0.0).astype(BF16))
            scores.append(per_sub)
            incs.append(lax.dot_general(pc["vb"][:, sl], pc["k_st"][:, sl], TN_DIMS,
                                        preferred_element_type=F32))
        return scores, incs

    def finish(pc, scores, incs):
        outs = []
        for hd, sl in enumerate(heads_sl):
            o_h = lax.dot_general(pc["q_in"][:, sl], st[hd].astype(BF16), NT_DIMS,
                                  preferred_element_type=F32)
            parts = [jnp.dot(scores[hd][j], pc["vb"][:(j + 1) * sub, sl],
                             preferred_element_type=F32) for j in range(nsub)]
            o_h = o_h + (parts[0] if nsub == 1 else jnp.concatenate(parts, axis=0))
            st[hd] = st[hd] * pc["e_last"][:, sl] + incs[hd]
            outs.append(_rms(o_h, og[:, sl]) * pc["gate"][:, sl])
        o_ref[0, pc["rows"], :] = jnp.concatenate(outs, axis=1).astype(o_ref.dtype)

    st = [st_ref[hd] for hd in range(heads)]
    prev = None
    for c in range(n_chunks):
        pc = prep(c)
        cur = (pc,) + state_free_products(pc)
        if prev is not None:
            finish(*prev)
        prev = cur
    finish(*prev)
    for hd in range(heads):
        st_ref[hd] = st[hd]

    @pl.when(t == nt - 1)
    def _():
        for hd in range(heads):
            sout_ref[0, hd] = st_ref[hd].T


def _hgrn_mixer(h, g, w_in, lb_logits, onorm_g, s0, *, layer, chunk, tm):
    bsz, seq, d = h.shape
    heads = d // HEAD_W
    sub = min(HG_SUB, chunk)
    kern = functools.partial(_hgrn_kernel, layer=layer, chunk=chunk, sub=sub, heads=heads)
    return pl.pallas_call(
        kern,
        grid=(bsz, seq // tm),
        in_specs=[
            pl.BlockSpec((1, tm, d), lambda b, t: (b, t, 0)),
            _const_spec((1, d)),
            _const_spec((d, 4 * d)),
            _const_spec(lb_logits.shape),
            _const_spec((1, d)),
            pl.BlockSpec((1, heads, HEAD_W, HEAD_W), lambda b, t: (b, 0, 0, 0)),
        ],
        out_specs=[
            pl.BlockSpec((1, tm, d), lambda b, t: (b, t, 0)),
            pl.BlockSpec((1, heads, HEAD_W, HEAD_W), lambda b, t: (b, 0, 0, 0)),
        ],
        out_shape=[
            jax.ShapeDtypeStruct((bsz, seq, d), BF16),
            jax.ShapeDtypeStruct((bsz, heads, HEAD_W, HEAD_W), F32),
        ],
        scratch_shapes=[pltpu.VMEM((heads, HEAD_W, HEAD_W), F32)],
        compiler_params=pltpu.CompilerParams(
            dimension_semantics=("parallel", "arbitrary"), vmem_limit_bytes=VMEM_LIMIT),
        name="hgrn_mixer",
    )(h, g.reshape(1, d), w_in, lb_logits, onorm_g.reshape(1, d), s0)


def _out_mlp_kernel(x_ref, h_ref, wo_ref, g1_ref, g2_ref, wu_ref, wd_ref, g3_ref, y_ref):
    m = jnp.dot(x_ref[...], wo_ref[...], preferred_element_type=F32)
    h1 = h_ref[...] + _rms(m, g1_ref[...])
    a2 = _rms(h1, g2_ref[...]).astype(BF16)
    d_ff = wu_ref.shape[1]
    acc = None
    for c in range(d_ff // D_FF_TILE):
        cs = slice(c * D_FF_TILE, (c + 1) * D_FF_TILE)
        u = jnp.dot(a2, wu_ref[:, cs], preferred_element_type=F32)
        u = jnp.maximum(u, 0.0)
        u = (u * u).astype(BF16)
        part = jnp.dot(u, wd_ref[cs, :], preferred_element_type=F32)
        acc = part if acc is None else acc + part
    y_ref[...] = h1 + _rms(acc, g3_ref[...])


def _out_mlp(x, h, w_o, g1, g2, w_up, w_down, g3, *, tm):
    n, d = h.shape
    d_ff = w_up.shape[1]
    row = lambda i: (i, 0)
    return pl.pallas_call(
        _out_mlp_kernel,
        grid=(n // tm,),
        in_specs=[
            pl.BlockSpec((tm, d), row),
            pl.BlockSpec((tm, d), row),
            _const_spec((d, d)),
            _const_spec((1, d)),
            _const_spec((1, d)),
            _const_spec((d, d_ff)),
            _const_spec((d_ff, d)),
            _const_spec((1, d)),
        ],
        out_specs=pl.BlockSpec((tm, d), row),
        out_shape=jax.ShapeDtypeStruct((n, d), F32),
        compiler_params=pltpu.CompilerParams(
            dimension_semantics=("parallel",), vmem_limit_bytes=VMEM_LIMIT),
        name="out_mlp",
    )(x, h, w_o, g1.reshape(1, d), g2.reshape(1, d), w_up, w_down, g3.reshape(1, d))


def _kvq_kernel(h_ref, gkv_ref, gq_ref, wkv_ref, wq_ref, k_ref, v_ref, kb_ref, vb_ref, qb_ref,
                *, k_transposed):
    h = h_ref[0]
    d = h.shape[1]
    akv = _rms(h, gkv_ref[...]).astype(BF16)
    kv = jnp.dot(akv, wkv_ref[...], preferred_element_type=F32)
    k = kv[:, :d]
    v = kv[:, d:]
    k_ref[0] = k.T if k_transposed else k
    v_ref[0] = v
    kb_ref[0] = k.astype(BF16)
    vb_ref[0] = v.astype(BF16)
    aq = _rms(h, gq_ref[...]).astype(BF16)
    q = jnp.dot(aq, wq_ref[...], preferred_element_type=F32)
    qb_ref[0] = (q * (DA_D ** -0.5 * LOG2_E)).astype(BF16)


def _kvq_proj(h, g_kv, g_q, w_kv, w_q, *, tm, k_transposed):
    bsz, seq, d = h.shape
    blk = pl.BlockSpec((1, tm, d), lambda b, t: (b, t, 0))
    if k_transposed:
        k_spec = pl.BlockSpec((1, d, tm), lambda b, t: (b, 0, t))
        k_shape = jax.ShapeDtypeStruct((bsz, d, seq), F32)
    else:
        k_spec, k_shape = blk, jax.ShapeDtypeStruct((bsz, seq, d), F32)
    return pl.pallas_call(
        functools.partial(_kvq_kernel, k_transposed=k_transposed),
        grid=(bsz, seq // tm),
        in_specs=[blk, _const_spec((1, d)), _const_spec((1, d)),
                  _const_spec((d, 2 * d)), _const_spec((d, d))],
        out_specs=[k_spec, blk, blk, blk, blk],
        out_shape=[k_shape, jax.ShapeDtypeStruct((bsz, seq, d), F32),
                   jax.ShapeDtypeStruct((bsz, seq, d), BF16), jax.ShapeDtypeStruct((bsz, seq, d), BF16),
                   jax.ShapeDtypeStruct((bsz, seq, d), BF16)],
        compiler_params=pltpu.CompilerParams(
            dimension_semantics=("parallel", "parallel"), vmem_limit_bytes=VMEM_LIMIT),
        name="kvq_proj",
    )(h, g_kv.reshape(1, d), g_q.reshape(1, d), w_kv, w_q)


def _lambda_value(lp, lam_init):
    a = jnp.sum(lp[0:1] * lp[1:2], axis=-1, keepdims=True)
    b = jnp.sum(lp[2:3] * lp[3:4], axis=-1, keepdims=True)
    return jnp.exp(a) - jnp.exp(b) + lam_init


def _attn_kernel(lam_ref, q_ref, k_ref, v_ref, sg_ref, o_ref,
                 vt_ref, s_ref, smax_ref, m_ref, acc_ref, *, blk, lam_init):
    i = pl.program_id(2)
    seq = k_ref.shape[1]

    @pl.when(i == 0)
    def _():
        def tr(c, carry):
            vb = v_ref[0, pl.ds(pl.multiple_of(c * blk, blk), blk), :].astype(F32)
            vt_ref[c, 0:HEAD_W, :] = vb.T.astype(BF16)
            vt_ref[c, HEAD_W:, :] = jnp.ones((VT_PAD, blk), BF16)
            return carry
        lax.fori_loop(0, seq // blk, tr, 0)

    q = q_ref[0]
    lane = lax.broadcasted_iota(jnp.int32, q.shape, 1)
    zero = jnp.zeros_like(q)
    qc = (jnp.where(lane < DA_D, q, zero), jnp.where(lane >= DA_D, q, zero))

    m_ref[...] = jnp.full(m_ref.shape, NEG_BIG, F32)
    acc_ref[...] = jnp.zeros(acc_ref.shape, F32)

    def scores(j, slot, masked):
        kb = k_ref[0, pl.ds(pl.multiple_of(j * blk, blk), blk), :]
        if masked:
            kr = lax.broadcasted_iota(jnp.int32, (blk, blk), 0) // MASK_CHUNK
            qr = lax.broadcasted_iota(jnp.int32, (blk, blk), 1) // MASK_CHUNK
            vis = kr <= qr
        for c in range(2):
            s = lax.dot_general(kb, qc[c], NT_DIMS, preferred_element_type=F32)
            if masked:
                s = jnp.where(vis, s, NEG_BIG)
            s_ref[slot, c] = s
            smax_ref[slot, c] = jnp.max(s, axis=0, keepdims=True)

    def consume(j, slot):
        vt = vt_ref[j]
        for c in range(2):
            s = s_ref[slot, c]
            m_old = m_ref[c:c + 1, :]
            m_new = jnp.maximum(m_old, smax_ref[slot, c])
            alpha = jnp.exp2(m_old - m_new)
            p = jnp.exp2(s - m_new).astype(BF16)
            m_ref[c:c + 1, :] = m_new
            acc_ref[c] = acc_ref[c] * alpha + jnp.dot(vt, p, preferred_element_type=F32)

    scores(i, 0, True)

    def body(t, carry):
        scores(2 * t, 1, False)
        consume(jnp.where(t == 0, i, 2 * t - 1), 0)
        scores(2 * t + 1, 0, False)
        consume(2 * t, 1)
        return carry
    lax.fori_loop(0, i // 2, body, 0)

    @pl.when(i % 2 == 1)
    def _():
        scores(i - 1, 1, False)
        consume(jnp.where(i == 1, i, i - 2), 0)
        consume(i - 1, 1)

    @pl.when(i % 2 == 0)
    def _():
        consume(jnp.maximum(i - 1, 0), 0)

    lam = _lambda_value(lam_ref[...], lam_init)
    o_t = (acc_ref[0, 0:HEAD_W, :] / acc_ref[0, HEAD_W:HEAD_W + 1, :]
           - lam * (acc_ref[1, 0:HEAD_W, :] / acc_ref[1, HEAD_W:HEAD_W + 1, :]))
    ms = jnp.mean(o_t * o_t, axis=0, keepdims=True)
    o_t = o_t * lax.rsqrt(ms + NORM_EPS)
    o = o_t.T * sg_ref[...] * (1.0 - lam_init)
    o_ref[0] = o.astype(o_ref.dtype)


def _attn_prompt(qb, kb, vb, lam_p, subln_g, *, lam_init, blk):
    bsz, seq, d = qb.shape
    heads = d // HEAD_W
    kern = functools.partial(_attn_kernel, blk=blk, lam_init=lam_init)
    qspec = pl.BlockSpec((1, blk, HEAD_W), lambda b, h, i: (b, i, h))
    kvspec = pl.BlockSpec((1, seq, HEAD_W), lambda b, h, i: (b, 0, h))
    return pl.pallas_call(
        kern,
        grid=(bsz, heads, seq // blk),
        in_specs=[pl.BlockSpec(lam_p.shape, lambda b, h, i: (0, 0)), qspec, kvspec, kvspec,
                  pl.BlockSpec((1, HEAD_W), lambda b, h, i: (0, 0))],
        out_specs=qspec,
        out_shape=jax.ShapeDtypeStruct((bsz, seq, d), BF16),
        scratch_shapes=[
            pltpu.VMEM((seq // blk, HEAD_W + VT_PAD, blk), BF16),
            pltpu.VMEM((2, 2, blk, blk), F32),
            pltpu.VMEM((2, 2, 1, blk), F32),
            pltpu.VMEM((2, blk), F32),
            pltpu.VMEM((2, HEAD_W + VT_PAD, blk), F32),
        ],
        compiler_params=pltpu.CompilerParams(
            dimension_semantics=("parallel", "parallel", "arbitrary"),
            vmem_limit_bytes=VMEM_LIMIT),
        name="diff_attn_prompt",
    )(lam_p, qb, kb, vb, subln_g.reshape(1, HEAD_W))


def _attn_cache_kernel(lam_ref, q_ref, kn_ref, vn_ref, pk_ref, pv_ref, sg_ref, o_ref,
                       *, lam_init, heads):
    tq = q_ref.shape[1]
    past = pk_ref.shape[1]
    lam = _lambda_value(lam_ref[...], lam_init)
    q = q_ref[0]
    lane = lax.broadcasted_iota(jnp.int32, (tq, HEAD_W), 1)
    qpos = (lax.broadcasted_iota(jnp.int32, (tq, past), 0) + past) // MASK_CHUNK
    vis_p = (lax.broadcasted_iota(jnp.int32, (tq, past), 1) // MASK_CHUNK) <= qpos
    qpos_n = (lax.broadcasted_iota(jnp.int32, (tq, tq), 0) + past) // MASK_CHUNK
    vis_n = ((lax.broadcasted_iota(jnp.int32, (tq, tq), 1) + past) // MASK_CHUNK) <= qpos_n
    outs = []
    for hd in range(heads):
        sl = slice(hd * HEAD_W, (hd + 1) * HEAD_W)
        qh = q[:, sl]
        zero = jnp.zeros_like(qh)
        kp = pk_ref[0, :, sl].astype(BF16)
        kn = kn_ref[0, :, sl]
        w_p, w_n = None, None
        for c in range(2):
            qc = jnp.where(lane < DA_D, qh, zero) if c == 0 else jnp.where(lane >= DA_D, qh, zero)
            sp = lax.dot_general(qc, kp, NT_DIMS, preferred_element_type=F32)
            sn = lax.dot_general(qc, kn, NT_DIMS, preferred_element_type=F32)
            sp = jnp.where(vis_p, sp, NEG_BIG)
            sn = jnp.where(vis_n, sn, NEG_BIG)
            m = jnp.maximum(jnp.max(sp, axis=-1, keepdims=True), jnp.max(sn, axis=-1, keepdims=True))
            pp = jnp.exp2(sp - m)
            pn = jnp.exp2(sn - m)
            den = jnp.sum(pp, axis=-1, keepdims=True) + jnp.sum(pn, axis=-1, keepdims=True)
            pp = pp / den
            pn = pn / den
            if c == 0:
                w_p, w_n = pp, pn
            else:
                w_p, w_n = w_p - lam * pp, w_n - lam * pn
        o_h = (jnp.dot(w_p.astype(BF16), pv_ref[0, :, sl].astype(BF16), preferred_element_type=F32)
               + jnp.dot(w_n.astype(BF16), vn_ref[0, :, sl], preferred_element_type=F32))
        outs.append(_rms(o_h, sg_ref[...]) * (1.0 - lam_init))
    o_ref[0] = jnp.concatenate(outs, axis=1).astype(o_ref.dtype)


def _attn_cache(qb, kb, vb, past_k, past_v, lam_p, subln_g, *, lam_init):
    bsz, tq, d = qb.shape
    past = past_k.shape[1]
    heads = d // HEAD_W
    kern = functools.partial(_attn_cache_kernel, lam_init=lam_init, heads=heads)
    new = pl.BlockSpec((1, tq, d), lambda b: (b, 0, 0))
    old = pl.BlockSpec((1, past, d), lambda b: (b, 0, 0))
    return pl.pallas_call(
        kern,
        grid=(bsz,),
        in_specs=[pl.BlockSpec(lam_p.shape, lambda b: (0, 0)), new, new, new, old, old,
                  pl.BlockSpec((1, HEAD_W), lambda b: (0, 0))],
        out_specs=new,
        out_shape=jax.ShapeDtypeStruct((bsz, tq, d), BF16),
        compiler_params=pltpu.CompilerParams(
            dimension_semantics=("parallel",), vmem_limit_bytes=VMEM_LIMIT),
        name="diff_attn_cache",
    )(lam_p, qb, kb, vb, past_k, past_v, subln_g.reshape(1, HEAD_W))


def _trunk(x, hg_state0, past_k, past_v, p, *, hg_chunk, hg_tm, row_tm, attn_blk):
    bsz, seq, d = x.shape
    n = bsz * seq
    depth = p["norm_g"].shape[0]
    n_a = p["w_hgrn_in"].shape[0]
    heads = d // HEAD_W
    h = x
    states = []
    k32 = v32 = kb = vb = None
    for l in range(depth):
        g = p["norm_g"][l]
        if l < n_a:
            o, s_new = _hgrn_mixer(h, g[0], p["w_hgrn_in"][l], p["hgrn_lb_logits"],
                                   p["hgrn_onorm_g"][l], hg_state0[l],
                                   layer=l, chunk=hg_chunk, tm=hg_tm)
            states.append(s_new)
            w_o = p["w_hgrn_out"][l]
        else:
            j = l - n_a
            if j == 0:
                k_t = past_k is None and seq % 128 == 0
                tm_kv = min(row_tm, seq)
                k32, v32, kb, vb, qb = _kvq_proj(h, p["kv_norm_g"], g[0], p["w_kv"], p["w_dq"][j],
                                                 tm=tm_kv, k_transposed=k_t)
            else:
                raise NotImplementedError("one attention layer per shared K/V is supported")
            lam_init = 0.8 - 0.6 * math.exp(-0.3 * l)
            if past_k is None:
                o = _attn_prompt(qb, kb, vb, p["diff_lambda"][j], p["diff_subln_g"][j],
                                 lam_init=lam_init, blk=attn_blk)
            else:
                o = _attn_cache(qb, kb, vb, past_k.reshape(bsz, -1, d), past_v.reshape(bsz, -1, d),
                                p["diff_lambda"][j], p["diff_subln_g"][j], lam_init=lam_init)
            w_o = p["w_do"][j]
        h = _out_mlp(o.reshape(n, d), h.reshape(n, d), w_o, g[1], g[2],
                     p["w_up"][l], p["w_down"][l], g[3], tm=row_tm).reshape(bsz, seq, d)
    if k_t:
        k_new = k32.reshape(bsz, heads, 2, DA_D, seq).transpose(0, 4, 1, 2, 3)
    else:
        k_new = k32.reshape(bsz, seq, heads, 2, DA_D)
    v_new = v32.reshape(bsz, seq, heads, HEAD_W)
    return h, k_new, v_new, jnp.stack(states)


def kernel(x_prompt, x_sample, cache_k, cache_v, state_hgrn, norm_g, w_hgrn_in, hgrn_lb_logits,
           hgrn_onorm_g, w_hgrn_out, kv_norm_g, w_kv, w_dq, diff_lambda, diff_subln_g, w_do,
           w_up, w_down):
    p = dict(
        norm_g=norm_g, hgrn_lb_logits=hgrn_lb_logits, hgrn_onorm_g=hgrn_onorm_g,
        kv_norm_g=kv_norm_g, diff_lambda=diff_lambda, diff_subln_g=diff_subln_g,
        w_hgrn_in=w_hgrn_in.astype(BF16), w_hgrn_out=w_hgrn_out.astype(BF16),
        w_kv=w_kv.astype(BF16), w_dq=w_dq.astype(BF16), w_do=w_do.astype(BF16),
        w_up=w_up.astype(BF16), w_down=w_down.astype(BF16),
    )
    n_a = w_hgrn_in.shape[0]
    bp, tp, d = x_prompt.shape
    bs, ts, _ = x_sample.shape
    heads = d // HEAD_W
    s0 = jnp.zeros((n_a, bp, heads, HEAD_W, HEAD_W), x_prompt.dtype)
    y_p, k_p, v_p, st_p = _trunk(x_prompt, s0, None, None, p, hg_chunk=HG_PROMPT_CHUNK,
                                 hg_tm=512, row_tm=512, attn_blk=512)
    y_s, k_s, v_s, st_s = _trunk(x_sample, state_hgrn, cache_k, cache_v, p, hg_chunk=ts,
                                 hg_tm=ts, row_tm=bs * ts, attn_blk=None)
    return (y_p, y_s, k_p, v_p, st_p, k_s, v_s, st_s)
```

```python
import functools
import math

import jax
import jax.numpy as jnp
from jax import lax
from jax.experimental import pallas as pl
from jax.experimental.pallas import tpu as pltpu

F32 = jnp.float32
BF16 = jnp.bfloat16

NORM_EPS = 1e-6
MASK_CHUNK = 64
HEAD_W = 128
DA_D = 64
HG_PROMPT_CHUNK = 64
HG_SUB = 32
EXP_CLAMP = 80.0
NEG_BIG = -1e30
D_FF_TILE = 512
VT_PAD = 16
F32_SUBLANES = 8
LOG2_E = math.log2(math.e)
VMEM_LIMIT = 56 * 1024 * 1024

NT_DIMS = (((1,), (1,)), ((), ()))
TN_DIMS = (((0,), (0,)), ((), ()))


def _rms(x, g):
    ms = jnp.mean(x * x, axis=-1, keepdims=True)
    return x * lax.rsqrt(ms + NORM_EPS) * g


def _sigmoid(x):
    return 1.0 / (1.0 + jnp.exp(-x))


def _const_spec(shape):
    nd = len(shape)
    return pl.BlockSpec(shape, lambda *_: (0,) * nd, pipeline_mode=pl.Buffered(1))


def _hgrn_kernel(h_ref, g_ref, w_ref, lbl_ref, og_ref, s0_ref, o_ref, sout_ref, st_ref,
                 *, layer, chunk, sub, heads):
    t = pl.program_id(1)
    nt = pl.num_programs(1)
    tm = h_ref.shape[1]
    d = h_ref.shape[2]

    @pl.when(t == 0)
    def _():
        for hd in range(heads):
            st_ref[hd] = s0_ref[0, hd].T

    lg = lbl_ref[...]
    lg = jnp.exp(lg - jnp.max(lg, axis=0, keepdims=True))
    lb = jnp.sum(lg[: layer + 1], axis=0, keepdims=True) / jnp.sum(lg, axis=0, keepdims=True)

    a = _rms(h_ref[0], g_ref[...]).astype(BF16)
    proj = jnp.dot(a, w_ref[...], preferred_element_type=F32)
    og = og_ref[...]

    r_i = lax.broadcasted_iota(jnp.int32, (chunk, chunk), 0)
    c_i = lax.broadcasted_iota(jnp.int32, (chunk, chunk), 1)
    tri = (c_i <= r_i).astype(BF16)

    nsub = chunk // sub
    n_chunks = tm // chunk
    heads_sl = [slice(hd * HEAD_W, (hd + 1) * HEAD_W) for hd in range(heads)]

    def prep(c):
        rows = slice(c * chunk, (c + 1) * chunk)
        qr = proj[rows, 0:d]
        fr = proj[rows, d:2 * d]
        vr = proj[rows, 2 * d:3 * d]
        gr = proj[rows, 3 * d:4 * d]

        fg = lb + (1.0 - lb) * _sigmoid(fr)
        gl = jnp.log(fg)
        p0 = gl.astype(BF16)
        r1 = gl - p0.astype(F32)
        p1 = r1.astype(BF16)
        p2 = (r1 - p1.astype(F32)).astype(BF16)
        b = (jnp.dot(tri, p0, preferred_element_type=F32)
             + jnp.dot(tri, p1, preferred_element_type=F32)
             + jnp.dot(tri, p2, preferred_element_type=F32))

        qh = qr * _sigmoid(qr)
        kh = 1.0 - fg
        vb = vr.astype(BF16)
        b_last = b[chunk - 1:chunk, :]
        q_in = (qh * jnp.exp(b)).astype(BF16)
        k_st = (kh * jnp.exp(b_last - b)).astype(BF16)
        e_last = jnp.exp(b_last)

        q_sub, k_sub = [], []
        for j in range(nsub):
            ref = b[j * sub + sub // 2 - 1: j * sub + sub // 2, :]
            ncol = (j + 1) * sub
            q_sub.append((qh[j * sub:ncol] * jnp.exp(jnp.minimum(b[j * sub:ncol] - ref, EXP_CLAMP))
                          ).astype(BF16))
            k_sub.append((kh[:ncol] * jnp.exp(jnp.minimum(ref - b[:ncol], EXP_CLAMP))).astype(BF16))

        gate = gr * _sigmoid(gr)
        return dict(rows=rows, q_in=q_in, k_st=k_st, e_last=e_last, q_sub=q_sub, k_sub=k_sub,
                    vb=vb, gate=gate)

    def state_free_products(pc):
        scores, incs = [], []
        for sl in heads_sl:
            per_sub = []
            for j in range(nsub):
                ncol = (j + 1) * sub
                sc = lax.dot_general(pc["q_sub"][j][:, sl], pc["k_sub"][j][:, sl], NT_DIMS,
                                     preferred_element_type=F32)
                rr = lax.broadcasted_iota(jnp.int32, (sub, ncol), 0) + j * sub
                cc = lax.broadcasted_iota(jnp.int32, (sub, ncol), 1)
                per_sub.append(jnp.where(cc <= rr, sc, 0.0).astype(BF16))
            scores.append(per_sub)
            incs.append(lax.dot_general(pc["vb"][:, sl], pc["k_st"][:, sl], TN_DIMS,
                                        preferred_element_type=F32))
        return scores, incs

    def finish(pc, scores, incs):
        outs = []
        for hd, sl in enumerate(heads_sl):
            o_h = lax.dot_general(pc["q_in"][:, sl], st[hd].astype(BF16), NT_DIMS,
                                  preferred_element_type=F32)
            parts = [jnp.dot(scores[hd][j], pc["vb"][:(j + 1) * sub, sl],
                             preferred_element_type=F32) for j in range(nsub)]
            o_h = o_h + (parts[0] if nsub == 1 else jnp.concatenate(parts, axis=0))
            st[hd] = st[hd] * pc["e_last"][:, sl] + incs[hd]
            outs.append(_rms(o_h, og[:, sl]) * pc["gate"][:, sl])
        o_ref[0, pc["rows"], :] = jnp.concatenate(outs, axis=1).astype(o_ref.dtype)

    st = [st_ref[hd] for hd in range(heads)]
    preps = [prep(0)]
    prev = None
    for c in range(n_chunks):
        if c + 1 < n_chunks:
            preps.append(prep(c + 1))
        pc = preps[c]
        cur = (pc,) + state_free_products(pc)
        if prev is not None:
            finish(*prev)
        prev = cur
    finish(*prev)
    for hd in range(heads):
        st_ref[hd] = st[hd]

    @pl.when(t == nt - 1)
    def _():
        for hd in range(heads):
            sout_ref[0, hd] = st_ref[hd].T


def _hgrn_mixer(h, g, w_in, lb_logits, onorm_g, s0, *, layer, chunk, tm):
    bsz, seq, d = h.shape
    heads = d // HEAD_W
    sub = min(HG_SUB, chunk)
    kern = functools.partial(_hgrn_kernel, layer=layer, chunk=chunk, sub=sub, heads=heads)
    return pl.pallas_call(
        kern,
        grid=(bsz, seq // tm),
        in_specs=[
            pl.BlockSpec((1, tm, d), lambda b, t: (b, t, 0)),
            _const_spec((1, d)),
            _const_spec((d, 4 * d)),
            _const_spec(lb_logits.shape),
            _const_spec((1, d)),
            pl.BlockSpec((1, heads, HEAD_W, HEAD_W), lambda b, t: (b, 0, 0, 0)),
        ],
        out_specs=[
            pl.BlockSpec((1, tm, d), lambda b, t: (b, t, 0)),
            pl.BlockSpec((1, heads, HEAD_W, HEAD_W), lambda b, t: (b, 0, 0, 0)),
        ],
        out_shape=[
            jax.ShapeDtypeStruct((bsz, seq, d), BF16),
            jax.ShapeDtypeStruct((bsz, heads, HEAD_W, HEAD_W), F32),
        ],
        scratch_shapes=[pltpu.VMEM((heads, HEAD_W, HEAD_W), F32)],
        compiler_params=pltpu.CompilerParams(
            dimension_semantics=("parallel", "arbitrary"), vmem_limit_bytes=VMEM_LIMIT),
        name="hgrn_mixer",
    )(h, g.reshape(1, d), w_in, lb_logits, onorm_g.reshape(1, d), s0)


def _out_mlp_kernel(x_ref, h_ref, wo_ref, g1_ref, g2_ref, wu_ref, wd_ref, g3_ref, y_ref):
    m = jnp.dot(x_ref[...], wo_ref[...], preferred_element_type=F32)
    h1 = h_ref[...] + _rms(m, g1_ref[...])
    a2 = _rms(h1, g2_ref[...]).astype(BF16)
    d_ff = wu_ref.shape[1]
    acc = None
    for c in range(d_ff // D_FF_TILE):
        cs = slice(c * D_FF_TILE, (c + 1) * D_FF_TILE)
        u = jnp.dot(a2, wu_ref[:, cs], preferred_element_type=F32)
        u = jnp.maximum(u, 0.0)
        u = (u * u).astype(BF16)
        part = jnp.dot(u, wd_ref[cs, :], preferred_element_type=F32)
        acc = part if acc is None else acc + part
    y_ref[...] = h1 + _rms(acc, g3_ref[...])


def _out_mlp(x, h, w_o, g1, g2, w_up, w_down, g3, *, tm):
    n, d = h.shape
    d_ff = w_up.shape[1]
    row = lambda i: (i, 0)
    return pl.pallas_call(
        _out_mlp_kernel,
        grid=(n // tm,),
        in_specs=[
            pl.BlockSpec((tm, d), row),
            pl.BlockSpec((tm, d), row),
            _const_spec((d, d)),
            _const_spec((1, d)),
            _const_spec((1, d)),
            _const_spec((d, d_ff)),
            _const_spec((d_ff, d)),
            _const_spec((1, d)),
        ],
        out_specs=pl.BlockSpec((tm, d), row),
        out_shape=jax.ShapeDtypeStruct((n, d), F32),
        compiler_params=pltpu.CompilerParams(
            dimension_semantics=("parallel",), vmem_limit_bytes=VMEM_LIMIT),
        name="out_mlp",
    )(x, h, w_o, g1.reshape(1, d), g2.reshape(1, d), w_up, w_down, g3.reshape(1, d))


def _kvq_kernel(h_ref, gkv_ref, gq_ref, wkv_ref, wq_ref, k_ref, v_ref, kb_ref, vb_ref, qb_ref,
                *, k_transposed):
    h = h_ref[0]
    d = h.shape[1]
    akv = _rms(h, gkv_ref[...]).astype(BF16)
    kv = jnp.dot(akv, wkv_ref[...], preferred_element_type=F32)
    k = kv[:, :d]
    v = kv[:, d:]
    k_ref[0] = k.T if k_transposed else k
    v_ref[0] = v
    kb_ref[0] = k.astype(BF16)
    vb_ref[0] = v.astype(BF16)
    aq = _rms(h, gq_ref[...]).astype(BF16)
    q = jnp.dot(aq, wq_ref[...], preferred_element_type=F32)
    qb_ref[0] = (q * (DA_D ** -0.5 * LOG2_E)).astype(BF16)


def _kvq_proj(h, g_kv, g_q, w_kv, w_q, *, tm, k_transposed):
    bsz, seq, d = h.shape
    blk = pl.BlockSpec((1, tm, d), lambda b, t: (b, t, 0))
    if k_transposed:
        k_spec = pl.BlockSpec((1, d, tm), lambda b, t: (b, 0, t))
        k_shape = jax.ShapeDtypeStruct((bsz, d, seq), F32)
    else:
        k_spec, k_shape = blk, jax.ShapeDtypeStruct((bsz, seq, d), F32)
    return pl.pallas_call(
        functools.partial(_kvq_kernel, k_transposed=k_transposed),
        grid=(bsz, seq // tm),
        in_specs=[blk, _const_spec((1, d)), _const_spec((1, d)),
                  _const_spec((d, 2 * d)), _const_spec((d, d))],
        out_specs=[k_spec, blk, blk, blk, blk],
        out_shape=[k_shape, jax.ShapeDtypeStruct((bsz, seq, d), F32),
                   jax.ShapeDtypeStruct((bsz, seq, d), BF16), jax.ShapeDtypeStruct((bsz, seq, d), BF16),
                   jax.ShapeDtypeStruct((bsz, seq, d), BF16)],
        compiler_params=pltpu.CompilerParams(
            dimension_semantics=("parallel", "parallel"), vmem_limit_bytes=VMEM_LIMIT),
        name="kvq_proj",
    )(h, g_kv.reshape(1, d), g_q.reshape(1, d), w_kv, w_q)


def _lambda_value(lp, lam_init):
    a = jnp.sum(lp[0:1] * lp[1:2], axis=-1, keepdims=True)
    b = jnp.sum(lp[2:3] * lp[3:4], axis=-1, keepdims=True)
    return jnp.exp(a) - jnp.exp(b) + lam_init


def _attn_kernel(q_ref, k_ref, v_ref, lam_ref, sg_ref, o_ref,
                 vt_ref, s_ref, smax_ref, m_ref, acc_ref, *, blk, lam_init):
    i = pl.program_id(2)
    seq = k_ref.shape[1]

    @pl.when(i == 0)
    def _():
        def tr(c, carry):
            vb = v_ref[0, pl.ds(pl.multiple_of(c * blk, blk), blk), :].astype(F32)
            vt_ref[c, 0:HEAD_W, :] = vb.T.astype(BF16)
            vt_ref[c, HEAD_W:, :] = jnp.ones((VT_PAD, blk), BF16)
            return carry
        lax.fori_loop(0, seq // blk, tr, 0)

    q = q_ref[0]
    lane = lax.broadcasted_iota(jnp.int32, q.shape, 1)
    zero = jnp.zeros_like(q)
    qc = (jnp.where(lane < DA_D, q, zero), jnp.where(lane >= DA_D, q, zero))

    m_ref[...] = jnp.full(m_ref.shape, NEG_BIG, F32)
    acc_ref[...] = jnp.zeros(acc_ref.shape, F32)

    def scores(j, slot, masked):
        kb = k_ref[0, pl.ds(pl.multiple_of(j * blk, blk), blk), :]
        if masked:
            kr = lax.broadcasted_iota(jnp.int32, (blk, blk), 0) // MASK_CHUNK
            qr = lax.broadcasted_iota(jnp.int32, (blk, blk), 1) // MASK_CHUNK
            vis = kr <= qr
        for c in range(2):
            s = lax.dot_general(kb, qc[c], NT_DIMS, preferred_element_type=F32)
            if masked:
                s = jnp.where(vis, s, NEG_BIG)
            s_ref[slot, c] = s
            smax_ref[slot, c, 0:1, :] = jnp.max(s, axis=0, keepdims=True)

    def consume(j, slot):
        vt = vt_ref[j]
        for c in range(2):
            s = s_ref[slot, c]
            m_old = m_ref[c, 0:1, :]
            m_new = jnp.maximum(m_old, smax_ref[slot, c, 0:1, :])
            alpha = jnp.exp2(m_old - m_new)
            p = jnp.exp2(s - m_new).astype(BF16)
            m_ref[c, 0:1, :] = m_new
            acc_ref[c] = acc_ref[c] * alpha + jnp.dot(vt, p, preferred_element_type=F32)

    scores(i, 0, True)

    def body(t, carry):
        scores(2 * t, 1, False)
        consume(jnp.where(t == 0, i, 2 * t - 1), 0)
        scores(2 * t + 1, 0, False)
        consume(2 * t, 1)
        return carry
    lax.fori_loop(0, i // 2, body, 0)

    @pl.when(i % 2 == 1)
    def _():
        scores(i - 1, 1, False)
        consume(jnp.where(i == 1, i, i - 2), 0)
        consume(i - 1, 1)

    @pl.when(i % 2 == 0)
    def _():
        consume(jnp.maximum(i - 1, 0), 0)

    lam = _lambda_value(lam_ref[...], lam_init)
    o_t = (acc_ref[0, 0:HEAD_W, :] / acc_ref[0, HEAD_W:HEAD_W + 1, :]
           - lam * (acc_ref[1, 0:HEAD_W, :] / acc_ref[1, HEAD_W:HEAD_W + 1, :]))
    ms = jnp.mean(o_t * o_t, axis=0, keepdims=True)
    o_t = o_t * lax.rsqrt(ms + NORM_EPS)
    o = o_t.T * sg_ref[...] * (1.0 - lam_init)
    o_ref[0] = o.astype(o_ref.dtype)


def _attn_prompt(qb, kb, vb, lam_p, subln_g, *, lam_init, blk):
    bsz, seq, d = qb.shape
    heads = d // HEAD_W
    kern = functools.partial(_attn_kernel, blk=blk, lam_init=lam_init)
    qspec = pl.BlockSpec((1, blk, HEAD_W), lambda b, h, i: (b, i, h))
    kvspec = pl.BlockSpec((1, seq, HEAD_W), lambda b, h, i: (b, 0, h))
    return pl.pallas_call(
        kern,
        grid=(bsz, heads, seq // blk),
        in_specs=[qspec, kvspec, kvspec, pl.BlockSpec(lam_p.shape, lambda b, h, i: (0, 0)),
                  pl.BlockSpec((1, HEAD_W), lambda b, h, i: (0, 0))],
        out_specs=qspec,
        out_shape=jax.ShapeDtypeStruct((bsz, seq, d), BF16),
        scratch_shapes=[
            pltpu.VMEM((seq // blk, HEAD_W + VT_PAD, blk), BF16),
            pltpu.VMEM((2, 2, blk, blk), F32),
            pltpu.VMEM((2, 2, F32_SUBLANES, blk), F32),
            pltpu.VMEM((2, F32_SUBLANES, blk), F32),
            pltpu.VMEM((2, HEAD_W + VT_PAD, blk), F32),
        ],
        compiler_params=pltpu.CompilerParams(
            dimension_semantics=("parallel", "parallel", "arbitrary"),
            vmem_limit_bytes=VMEM_LIMIT),
        name="diff_attn_prompt",
    )(qb, kb, vb, lam_p, subln_g.reshape(1, HEAD_W))


def _attn_cache_kernel(lam_ref, q_ref, kn_ref, vn_ref, pk_ref, pv_ref, sg_ref, o_ref,
                       *, lam_init, heads):
    tq = q_ref.shape[1]
    past = pk_ref.shape[1]
    lam = _lambda_value(lam_ref[...], lam_init)
    q = q_ref[0]
    lane = lax.broadcasted_iota(jnp.int32, (tq, HEAD_W), 1)
    qpos = (lax.broadcasted_iota(jnp.int32, (tq, past), 0) + past) // MASK_CHUNK
    vis_p = (lax.broadcasted_iota(jnp.int32, (tq, past), 1) // MASK_CHUNK) <= qpos
    qpos_n = (lax.broadcasted_iota(jnp.int32, (tq, tq), 0) + past) // MASK_CHUNK
    vis_n = ((lax.broadcasted_iota(jnp.int32, (tq, tq), 1) + past) // MASK_CHUNK) <= qpos_n
    outs = []
    for hd in range(heads):
        sl = slice(hd * HEAD_W, (hd + 1) * HEAD_W)
        qh = q[:, sl]
        zero = jnp.zeros_like(qh)
        kp = pk_ref[0, :, sl].astype(BF16)
        kn = kn_ref[0, :, sl]
        w_p, w_n = None, None
        for c in range(2):
            qc = jnp.where(lane < DA_D, qh, zero) if c == 0 else jnp.where(lane >= DA_D, qh, zero)
            sp = lax.dot_general(qc, kp, NT_DIMS, preferred_element_type=F32)
            sn = lax.dot_general(qc, kn, NT_DIMS, preferred_element_type=F32)
            sp = jnp.where(vis_p, sp, NEG_BIG)
            sn = jnp.where(vis_n, sn, NEG_BIG)
            m = jnp.maximum(jnp.max(sp, axis=-1, keepdims=True), jnp.max(sn, axis=-1, keepdims=True))
            pp = jnp.exp2(sp - m)
            pn = jnp.exp2(sn - m)
            den = jnp.sum(pp, axis=-1, keepdims=True) + jnp.sum(pn, axis=-1, keepdims=True)
            pp = pp / den
            pn = pn / den
            if c == 0:
                w_p, w_n = pp, pn
            else:
                w_p, w_n = w_p - lam * pp, w_n - lam * pn
        o_h = (jnp.dot(w_p.astype(BF16), pv_ref[0, :, sl].astype(BF16), preferred_element_type=F32)
               + jnp.dot(w_n.astype(BF16), vn_ref[0, :, sl], preferred_element_type=F32))
        outs.append(_rms(o_h, sg_ref[...]) * (1.0 - lam_init))
    o_ref[0] = jnp.concatenate(outs, axis=1).astype(o_ref.dtype)


def _attn_cache(qb, kb, vb, past_k, past_v, lam_p, subln_g, *, lam_init):
    bsz, tq, d = qb.shape
    past = past_k.shape[1]
    heads = d // HEAD_W
    kern = functools.partial(_attn_cache_kernel, lam_init=lam_init, heads=heads)
    new = pl.BlockSpec((1, tq, d), lambda b: (b, 0, 0))
    old = pl.BlockSpec((1, past, d), lambda b: (b, 0, 0))
    return pl.pallas_call(
        kern,
        grid=(bsz,),
        in_specs=[pl.BlockSpec(lam_p.shape, lambda b: (0, 0)), new, new, new, old, old,
                  pl.BlockSpec((1, HEAD_W), lambda b: (0, 0))],
        out_specs=new,
        out_shape=jax.ShapeDtypeStruct((bsz, tq, d), BF16),
        compiler_params=pltpu.CompilerParams(
            dimension_semantics=("parallel",), vmem_limit_bytes=VMEM_LIMIT),
        name="diff_attn_cache",
    )(lam_p, qb, kb, vb, past_k, past_v, subln_g.reshape(1, HEAD_W))


def _trunk(x, hg_state0, past_k, past_v, p, *, hg_chunk, hg_tm, row_tm, attn_blk):
    bsz, seq, d = x.shape
    n = bsz * seq
    depth = p["norm_g"].shape[0]
    n_a = p["w_hgrn_in"].shape[0]
    heads = d // HEAD_W
    h = x
    states = []
    k32 = v32 = kb = vb = None
    for l in range(depth):
        g = p["norm_g"][l]
        if l < n_a:
            o, s_new = _hgrn_mixer(h, g[0], p["w_hgrn_in"][l], p["hgrn_lb_logits"],
                                   p["hgrn_onorm_g"][l], hg_state0[l],
                                   layer=l, chunk=hg_chunk, tm=hg_tm)
            states.append(s_new)
            w_o = p["w_hgrn_out"][l]
        else:
            j = l - n_a
            if j == 0:
                k_t = past_k is None and seq % 128 == 0
                tm_kv = min(row_tm, seq)
                k32, v32, kb, vb, qb = _kvq_proj(h, p["kv_norm_g"], g[0], p["w_kv"], p["w_dq"][j],
                                                 tm=tm_kv, k_transposed=k_t)
            else:
                raise NotImplementedError("one attention layer per shared K/V is supported")
            lam_init = 0.8 - 0.6 * math.exp(-0.3 * l)
            if past_k is None:
                o = _attn_prompt(qb, kb, vb, p["diff_lambda"][j], p["diff_subln_g"][j],
                                 lam_init=lam_init, blk=attn_blk)
            else:
                o = _attn_cache(qb, kb, vb, past_k.reshape(bsz, -1, d), past_v.reshape(bsz, -1, d),
                                p["diff_lambda"][j], p["diff_subln_g"][j], lam_init=lam_init)
            w_o = p["w_do"][j]
        h = _out_mlp(o.reshape(n, d), h.reshape(n, d), w_o, g[1], g[2],
                     p["w_up"][l], p["w_down"][l], g[3], tm=row_tm).reshape(bsz, seq, d)
    if k_t:
        k_new = k32.reshape(bsz, heads, 2, DA_D, seq).transpose(0, 4, 1, 2, 3)
    else:
        k_new = k32.reshape(bsz, seq, heads, 2, DA_D)
    v_new = v32.reshape(bsz, seq, heads, HEAD_W)
    return h, k_new, v_new, jnp.stack(states)


def kernel(x_prompt, x_sample, cache_k, cache_v, state_hgrn, norm_g, w_hgrn_in, hgrn_lb_logits,
           hgrn_onorm_g, w_hgrn_out, kv_norm_g, w_kv, w_dq, diff_lambda, diff_subln_g, w_do,
           w_up, w_down):
    p = dict(
        norm_g=norm_g, hgrn_lb_logits=hgrn_lb_logits, hgrn_onorm_g=hgrn_onorm_g,
        kv_norm_g=kv_norm_g, diff_lambda=diff_lambda, diff_subln_g=diff_subln_g,
        w_hgrn_in=w_hgrn_in.astype(BF16), w_hgrn_out=w_hgrn_out.astype(BF16),
        w_kv=w_kv.astype(BF16), w_dq=w_dq.astype(BF16), w_do=w_do.astype(BF16),
        w_up=w_up.astype(BF16), w_down=w_down.astype(BF16),
    )
    n_a = w_hgrn_in.shape[0]
    bp, tp, d = x_prompt.shape
    bs, ts, _ = x_sample.shape
    heads = d // HEAD_W
    s0 = jnp.zeros((n_a, bp, heads, HEAD_W, HEAD_W), x_prompt.dtype)
    y_p, k_p, v_p, st_p = _trunk(x_prompt, s0, None, None, p, hg_chunk=HG_PROMPT_CHUNK,
                                 hg_tm=512, row_tm=512, attn_blk=512)
    y_s, k_s, v_s, st_s = _trunk(x_sample, state_hgrn, cache_k, cache_v, p, hg_chunk=ts,
                                 hg_tm=ts, row_tm=bs * ts, attn_blk=None)
    return (y_p, y_s, k_p, v_p, st_p, k_s, v_s, st_s)
```

```python
import functools
import math

import jax
import jax.numpy as jnp
from jax import lax
from jax.experimental import pallas as pl
from jax.experimental.pallas import tpu as pltpu

F32 = jnp.float32
BF16 = jnp.bfloat16

NORM_EPS = 1e-6
MASK_CHUNK = 64
HEAD_W = 128
DA_D = 64
HG_PROMPT_CHUNK = 64
HG_SUB = 32
EXP_CLAMP = 80.0
NEG_BIG = -1e30
D_FF_TILE = 512
VT_PAD = 16
F32_SUBLANES = 8
LOG2_E = math.log2(math.e)
VMEM_LIMIT = 56 * 1024 * 1024

NT_DIMS = (((1,), (1,)), ((), ()))
TN_DIMS = (((0,), (0,)), ((), ()))


def _rms(x, g):
    ms = jnp.mean(x * x, axis=-1, keepdims=True)
    return x * lax.rsqrt(ms + NORM_EPS) * g


def _sigmoid(x):
    return 1.0 / (1.0 + jnp.exp(-x))


def _const_spec(shape):
    nd = len(shape)
    return pl.BlockSpec(shape, lambda *_: (0,) * nd, pipeline_mode=pl.Buffered(1))


def _hgrn_kernel(h_ref, g_ref, w_ref, lbl_ref, og_ref, s0_ref, o_ref, sout_ref, st_ref,
                 *, layer, chunk, sub, heads):
    t = pl.program_id(1)
    nt = pl.num_programs(1)
    tm = h_ref.shape[1]
    d = h_ref.shape[2]

    @pl.when(t == 0)
    def _():
        for hd in range(heads):
            st_ref[hd] = s0_ref[0, hd].T

    lg = lbl_ref[...]
    lg = jnp.exp(lg - jnp.max(lg, axis=0, keepdims=True))
    lb = jnp.sum(lg[: layer + 1], axis=0, keepdims=True) / jnp.sum(lg, axis=0, keepdims=True)

    a = _rms(h_ref[0], g_ref[...]).astype(BF16)
    proj = jnp.dot(a, w_ref[...], preferred_element_type=F32)
    og = og_ref[...]

    r_i = lax.broadcasted_iota(jnp.int32, (chunk, chunk), 0)
    c_i = lax.broadcasted_iota(jnp.int32, (chunk, chunk), 1)
    tri = (c_i <= r_i).astype(BF16)

    nsub = chunk // sub
    n_chunks = tm // chunk
    heads_sl = [slice(hd * HEAD_W, (hd + 1) * HEAD_W) for hd in range(heads)]

    def prep(c):
        rows = slice(c * chunk, (c + 1) * chunk)
        qr = proj[rows, 0:d]
        fr = proj[rows, d:2 * d]
        vr = proj[rows, 2 * d:3 * d]
        gr = proj[rows, 3 * d:4 * d]

        fg = lb + (1.0 - lb) * _sigmoid(fr)
        gl = jnp.log(fg)
        p0 = gl.astype(BF16)
        r1 = gl - p0.astype(F32)
        p1 = r1.astype(BF16)
        p2 = (r1 - p1.astype(F32)).astype(BF16)
        b = (jnp.dot(tri, p0, preferred_element_type=F32)
             + jnp.dot(tri, p1, preferred_element_type=F32)
             + jnp.dot(tri, p2, preferred_element_type=F32))

        qh = qr * _sigmoid(qr)
        kh = 1.0 - fg
        vb = vr.astype(BF16)
        b_last = b[chunk - 1:chunk, :]
        q_in = (qh * jnp.exp(b)).astype(BF16)
        k_st = (kh * jnp.exp(b_last - b)).astype(BF16)
        e_last = jnp.exp(b_last)

        q_sub, k_sub = [], []
        for j in range(nsub):
            ref = b[j * sub + sub // 2 - 1: j * sub + sub // 2, :]
            ncol = (j + 1) * sub
            q_sub.append((qh[j * sub:ncol] * jnp.exp(jnp.minimum(b[j * sub:ncol] - ref, EXP_CLAMP))
                          ).astype(BF16))
            k_sub.append((kh[:ncol] * jnp.exp(jnp.minimum(ref - b[:ncol], EXP_CLAMP))).astype(BF16))

        gate = gr * _sigmoid(gr)
        return dict(rows=rows, q_in=q_in, k_st=k_st, e_last=e_last, q_sub=q_sub, k_sub=k_sub,
                    vb=vb, gate=gate)

    def state_free_products(pc):
        scores, incs = [], []
        for sl in heads_sl:
            per_sub = []
            for j in range(nsub):
                ncol = (j + 1) * sub
                sc = lax.dot_general(pc["q_sub"][j][:, sl], pc["k_sub"][j][:, sl], NT_DIMS,
                                     preferred_element_type=F32)
                rr = lax.broadcasted_iota(jnp.int32, (sub, ncol), 0) + j * sub
                cc = lax.broadcasted_iota(jnp.int32, (sub, ncol), 1)
                per_sub.append(jnp.where(cc <= rr, sc, 0.0).astype(BF16))
            scores.append(per_sub)
            incs.append(lax.dot_general(pc["vb"][:, sl], pc["k_st"][:, sl], TN_DIMS,
                                        preferred_element_type=F32))
        return scores, incs

    def finish(pc, scores, incs):
        outs = []
        for hd, sl in enumerate(heads_sl):
            o_h = lax.dot_general(pc["q_in"][:, sl], st[hd].astype(BF16), NT_DIMS,
                                  preferred_element_type=F32)
            parts = [jnp.dot(scores[hd][j], pc["vb"][:(j + 1) * sub, sl],
                             preferred_element_type=F32) for j in range(nsub)]
            o_h = o_h + (parts[0] if nsub == 1 else jnp.concatenate(parts, axis=0))
            st[hd] = st[hd] * pc["e_last"][:, sl] + incs[hd]
            outs.append(_rms(o_h, og[:, sl]) * pc["gate"][:, sl])
        o_ref[0, pc["rows"], :] = jnp.concatenate(outs, axis=1).astype(o_ref.dtype)

    st = [st_ref[hd] for hd in range(heads)]
    preps = [prep(0)]
    prev = None
    for c in range(n_chunks):
        if c + 1 < n_chunks:
            preps.append(prep(c + 1))
        pc = preps[c]
        cur = (pc,) + state_free_products(pc)
        if prev is not None:
            finish(*prev)
        prev = cur
    finish(*prev)
    for hd in range(heads):
        st_ref[hd] = st[hd]

    @pl.when(t == nt - 1)
    def _():
        for hd in range(heads):
            sout_ref[0, hd] = st_ref[hd].T


def _hgrn_mixer(h, g, w_in, lb_logits, onorm_g, s0, *, layer, chunk, tm):
    bsz, seq, d = h.shape
    heads = d // HEAD_W
    sub = min(HG_SUB, chunk)
    kern = functools.partial(_hgrn_kernel, layer=layer, chunk=chunk, sub=sub, heads=heads)
    return pl.pallas_call(
        kern,
        grid=(bsz, seq // tm),
        in_specs=[
            pl.BlockSpec((1, tm, d), lambda b, t: (b, t, 0)),
            _const_spec((1, d)),
            _const_spec((d, 4 * d)),
            _const_spec(lb_logits.shape),
            _const_spec((1, d)),
            pl.BlockSpec((1, heads, HEAD_W, HEAD_W), lambda b, t: (b, 0, 0, 0)),
        ],
        out_specs=[
            pl.BlockSpec((1, tm, d), lambda b, t: (b, t, 0)),
            pl.BlockSpec((1, heads, HEAD_W, HEAD_W), lambda b, t: (b, 0, 0, 0)),
        ],
        out_shape=[
            jax.ShapeDtypeStruct((bsz, seq, d), BF16),
            jax.ShapeDtypeStruct((bsz, heads, HEAD_W, HEAD_W), F32),
        ],
        scratch_shapes=[pltpu.VMEM((heads, HEAD_W, HEAD_W), F32)],
        compiler_params=pltpu.CompilerParams(
            dimension_semantics=("parallel", "arbitrary"), vmem_limit_bytes=VMEM_LIMIT),
        name="hgrn_mixer",
    )(h, g.reshape(1, d), w_in, lb_logits, onorm_g.reshape(1, d), s0)


def _out_mlp_kernel(x_ref, h_ref, wo_ref, g1_ref, g2_ref, wu_ref, wd_ref, g3_ref, y_ref):
    m = jnp.dot(x_ref[...], wo_ref[...], preferred_element_type=F32)
    h1 = h_ref[...] + _rms(m, g1_ref[...])
    a2 = _rms(h1, g2_ref[...]).astype(BF16)
    d_ff = wu_ref.shape[1]
    acc = None
    for c in range(d_ff // D_FF_TILE):
        cs = slice(c * D_FF_TILE, (c + 1) * D_FF_TILE)
        u = jnp.dot(a2, wu_ref[:, cs], preferred_element_type=F32)
        u = jnp.maximum(u, 0.0)
        u = (u * u).astype(BF16)
        part = jnp.dot(u, wd_ref[cs, :], preferred_element_type=F32)
        acc = part if acc is None else acc + part
    y_ref[...] = h1 + _rms(acc, g3_ref[...])


def _out_mlp(x, h, w_o, g1, g2, w_up, w_down, g3, *, tm):
    n, d = h.shape
    d_ff = w_up.shape[1]
    row = lambda i: (i, 0)
    return pl.pallas_call(
        _out_mlp_kernel,
        grid=(n // tm,),
        in_specs=[
            pl.BlockSpec((tm, d), row),
            pl.BlockSpec((tm, d), row),
            _const_spec((d, d)),
            _const_spec((1, d)),
            _const_spec((1, d)),
            _const_spec((d, d_ff)),
            _const_spec((d_ff, d)),
            _const_spec((1, d)),
        ],
        out_specs=pl.BlockSpec((tm, d), row),
        out_shape=jax.ShapeDtypeStruct((n, d), F32),
        compiler_params=pltpu.CompilerParams(
            dimension_semantics=("parallel",), vmem_limit_bytes=VMEM_LIMIT),
        name="out_mlp",
    )(x, h, w_o, g1.reshape(1, d), g2.reshape(1, d), w_up, w_down, g3.reshape(1, d))


def _kvq_kernel(h_ref, gkv_ref, gq_ref, wkv_ref, wq_ref, k_ref, v_ref, kb_ref, vb_ref, qb_ref,
                *, k_transposed):
    h = h_ref[0]
    d = h.shape[1]
    akv = _rms(h, gkv_ref[...]).astype(BF16)
    kv = jnp.dot(akv, wkv_ref[...], preferred_element_type=F32)
    k = kv[:, :d]
    v = kv[:, d:]
    k_ref[0] = k.T if k_transposed else k
    v_ref[0] = v
    kb_ref[0] = k.astype(BF16)
    vb_ref[0] = v.astype(BF16)
    aq = _rms(h, gq_ref[...]).astype(BF16)
    q = jnp.dot(aq, wq_ref[...], preferred_element_type=F32)
    qb_ref[0] = (q * (DA_D ** -0.5 * LOG2_E)).astype(BF16)


def _kvq_proj(h, g_kv, g_q, w_kv, w_q, *, tm, k_transposed):
    bsz, seq, d = h.shape
    blk = pl.BlockSpec((1, tm, d), lambda b, t: (b, t, 0))
    if k_transposed:
        k_spec = pl.BlockSpec((1, d, tm), lambda b, t: (b, 0, t))
        k_shape = jax.ShapeDtypeStruct((bsz, d, seq), F32)
    else:
        k_spec, k_shape = blk, jax.ShapeDtypeStruct((bsz, seq, d), F32)
    return pl.pallas_call(
        functools.partial(_kvq_kernel, k_transposed=k_transposed),
        grid=(bsz, seq // tm),
        in_specs=[blk, _const_spec((1, d)), _const_spec((1, d)),
                  _const_spec((d, 2 * d)), _const_spec((d, d))],
        out_specs=[k_spec, blk, blk, blk, blk],
        out_shape=[k_shape, jax.ShapeDtypeStruct((bsz, seq, d), F32),
                   jax.ShapeDtypeStruct((bsz, seq, d), BF16), jax.ShapeDtypeStruct((bsz, seq, d), BF16),
                   jax.ShapeDtypeStruct((bsz, seq, d), BF16)],
        compiler_params=pltpu.CompilerParams(
            dimension_semantics=("parallel", "parallel"), vmem_limit_bytes=VMEM_LIMIT),
        name="kvq_proj",
    )(h, g_kv.reshape(1, d), g_q.reshape(1, d), w_kv, w_q)


def _lambda_value(lp, lam_init):
    a = jnp.sum(lp[0:1] * lp[1:2], axis=-1, keepdims=True)
    b = jnp.sum(lp[2:3] * lp[3:4], axis=-1, keepdims=True)
    return jnp.exp(a) - jnp.exp(b) + lam_init


def _attn_kernel(q_ref, k_ref, v_ref, lam_ref, sg_ref, o_ref,
                 vt_ref, s_ref, smax_ref, m_ref, acc_ref, *, blk, lam_init):
    i = pl.program_id(2)
    seq = k_ref.shape[1]

    @pl.when(i == 0)
    def _():
        def tr(c, carry):
            vb = v_ref[0, pl.ds(pl.multiple_of(c * blk, blk), blk), :].astype(F32)
            vt_ref[c, 0:HEAD_W, :] = vb.T.astype(BF16)
            vt_ref[c, HEAD_W:, :] = jnp.ones((VT_PAD, blk), BF16)
            return carry
        lax.fori_loop(0, seq // blk, tr, 0)

    qt = q_ref[0].astype(F32).T
    row = lax.broadcasted_iota(jnp.int32, qt.shape, 0)
    qc = (jnp.where(row < DA_D, qt, 0.0).astype(BF16), jnp.where(row >= DA_D, qt, 0.0).astype(BF16))

    m_ref[...] = jnp.full(m_ref.shape, NEG_BIG, F32)
    acc_ref[...] = jnp.zeros(acc_ref.shape, F32)

    def scores(j, slot, masked):
        kb = k_ref[0, pl.ds(pl.multiple_of(j * blk, blk), blk), :]
        if masked:
            kr = lax.broadcasted_iota(jnp.int32, (blk, blk), 0) // MASK_CHUNK
            qr = lax.broadcasted_iota(jnp.int32, (blk, blk), 1) // MASK_CHUNK
            vis = kr <= qr
        for c in range(2):
            s = jnp.dot(kb, qc[c], preferred_element_type=F32)
            if masked:
                s = jnp.where(vis, s, NEG_BIG)
            s_ref[slot, c] = s
            smax_ref[slot, c, 0:1, :] = jnp.max(s, axis=0, keepdims=True)

    def consume(j, slot):
        vt = vt_ref[j]
        for c in range(2):
            s = s_ref[slot, c]
            m_old = m_ref[c, 0:1, :]
            m_new = jnp.maximum(m_old, smax_ref[slot, c, 0:1, :])
            alpha = jnp.exp2(m_old - m_new)
            p = jnp.exp2(s - m_new).astype(BF16)
            m_ref[c, 0:1, :] = m_new
            acc_ref[c] = acc_ref[c] * alpha + jnp.dot(vt, p, preferred_element_type=F32)

    scores(i, 0, True)

    def body(t, carry):
        scores(2 * t, 1, False)
        consume(jnp.where(t == 0, i, 2 * t - 1), 0)
        scores(2 * t + 1, 0, False)
        consume(2 * t, 1)
        return carry
    lax.fori_loop(0, i // 2, body, 0)

    @pl.when(i % 2 == 1)
    def _():
        scores(i - 1, 1, False)
        consume(jnp.where(i == 1, i, i - 2), 0)
        consume(i - 1, 1)

    @pl.when(i % 2 == 0)
    def _():
        consume(jnp.maximum(i - 1, 0), 0)

    lam = _lambda_value(lam_ref[...], lam_init)
    o_t = (acc_ref[0, 0:HEAD_W, :] / acc_ref[0, HEAD_W:HEAD_W + 1, :]
           - lam * (acc_ref[1, 0:HEAD_W, :] / acc_ref[1, HEAD_W:HEAD_W + 1, :]))
    ms = jnp.mean(o_t * o_t, axis=0, keepdims=True)
    o_t = o_t * lax.rsqrt(ms + NORM_EPS)
    o = o_t.T * sg_ref[...] * (1.0 - lam_init)
    o_ref[0] = o.astype(o_ref.dtype)


def _attn_prompt(qb, kb, vb, lam_p, subln_g, *, lam_init, blk):
    bsz, seq, d = qb.shape
    heads = d // HEAD_W
    kern = functools.partial(_attn_kernel, blk=blk, lam_init=lam_init)
    qspec = pl.BlockSpec((1, blk, HEAD_W), lambda b, h, i: (b, i, h))
    kvspec = pl.BlockSpec((1, seq, HEAD_W), lambda b, h, i: (b, 0, h))
    return pl.pallas_call(
        kern,
        grid=(bsz, heads, seq // blk),
        in_specs=[qspec, kvspec, kvspec, pl.BlockSpec(lam_p.shape, lambda b, h, i: (0, 0)),
                  pl.BlockSpec((1, HEAD_W), lambda b, h, i: (0, 0))],
        out_specs=qspec,
        out_shape=jax.ShapeDtypeStruct((bsz, seq, d), BF16),
        scratch_shapes=[
            pltpu.VMEM((seq // blk, HEAD_W + VT_PAD, blk), BF16),
            pltpu.VMEM((2, 2, blk, blk), F32),
            pltpu.VMEM((2, 2, F32_SUBLANES, blk), F32),
            pltpu.VMEM((2, F32_SUBLANES, blk), F32),
            pltpu.VMEM((2, HEAD_W + VT_PAD, blk), F32),
        ],
        compiler_params=pltpu.CompilerParams(
            dimension_semantics=("parallel", "parallel", "arbitrary"),
            vmem_limit_bytes=VMEM_LIMIT),
        name="diff_attn_prompt",
    )(qb, kb, vb, lam_p, subln_g.reshape(1, HEAD_W))


def _attn_cache_kernel(lam_ref, q_ref, kn_ref, vn_ref, pk_ref, pv_ref, sg_ref, o_ref,
                       *, lam_init, heads):
    tq = q_ref.shape[1]
    past = pk_ref.shape[1]
    lam = _lambda_value(lam_ref[...], lam_init)
    q = q_ref[0]
    lane = lax.broadcasted_iota(jnp.int32, (tq, HEAD_W), 1)
    qpos = (lax.broadcasted_iota(jnp.int32, (tq, past), 0) + past) // MASK_CHUNK
    vis_p = (lax.broadcasted_iota(jnp.int32, (tq, past), 1) // MASK_CHUNK) <= qpos
    qpos_n = (lax.broadcasted_iota(jnp.int32, (tq, tq), 0) + past) // MASK_CHUNK
    vis_n = ((lax.broadcasted_iota(jnp.int32, (tq, tq), 1) + past) // MASK_CHUNK) <= qpos_n
    outs = []
    for hd in range(heads):
        sl = slice(hd * HEAD_W, (hd + 1) * HEAD_W)
        qh = q[:, sl]
        zero = jnp.zeros_like(qh)
        kp = pk_ref[0, :, sl].astype(BF16)
        kn = kn_ref[0, :, sl]
        w_p, w_n = None, None
        for c in range(2):
            qc = jnp.where(lane < DA_D, qh, zero) if c == 0 else jnp.where(lane >= DA_D, qh, zero)
            sp = lax.dot_general(qc, kp, NT_DIMS, preferred_element_type=F32)
            sn = lax.dot_general(qc, kn, NT_DIMS, preferred_element_type=F32)
            sp = jnp.where(vis_p, sp, NEG_BIG)
            sn = jnp.where(vis_n, sn, NEG_BIG)
            m = jnp.maximum(jnp.max(sp, axis=-1, keepdims=True), jnp.max(sn, axis=-1, keepdims=True))
            pp = jnp.exp2(sp - m)
            pn = jnp.exp2(sn - m)
            den = jnp.sum(pp, axis=-1, keepdims=True) + jnp.sum(pn, axis=-1, keepdims=True)
            pp = pp / den
            pn = pn / den
            if c == 0:
                w_p, w_n = pp, pn
            else:
                w_p, w_n = w_p - lam * pp, w_n - lam * pn
        o_h = (jnp.dot(w_p.astype(BF16), pv_ref[0, :, sl].astype(BF16), preferred_element_type=F32)
               + jnp.dot(w_n.astype(BF16), vn_ref[0, :, sl], preferred_element_type=F32))
        outs.append(_rms(o_h, sg_ref[...]) * (1.0 - lam_init))
    o_ref[0] = jnp.concatenate(outs, axis=1).astype(o_ref.dtype)


def _attn_cache(qb, kb, vb, past_k, past_v, lam_p, subln_g, *, lam_init):
    bsz, tq, d = qb.shape
    past = past_k.shape[1]
    heads = d // HEAD_W
    kern = functools.partial(_attn_cache_kernel, lam_init=lam_init, heads=heads)
    new = pl.BlockSpec((1, tq, d), lambda b: (b, 0, 0))
    old = pl.BlockSpec((1, past, d), lambda b: (b, 0, 0))
    return pl.pallas_call(
        kern,
        grid=(bsz,),
        in_specs=[pl.BlockSpec(lam_p.shape, lambda b: (0, 0)), new, new, new, old, old,
                  pl.BlockSpec((1, HEAD_W), lambda b: (0, 0))],
        out_specs=new,
        out_shape=jax.ShapeDtypeStruct((bsz, tq, d), BF16),
        compiler_params=pltpu.CompilerParams(
            dimension_semantics=("parallel",), vmem_limit_bytes=VMEM_LIMIT),
        name="diff_attn_cache",
    )(lam_p, qb, kb, vb, past_k, past_v, subln_g.reshape(1, HEAD_W))


def _trunk(x, hg_state0, past_k, past_v, p, *, hg_chunk, hg_tm, row_tm, attn_blk):
    bsz, seq, d = x.shape
    n = bsz * seq
    depth = p["norm_g"].shape[0]
    n_a = p["w_hgrn_in"].shape[0]
    heads = d // HEAD_W
    h = x
    states = []
    k32 = v32 = kb = vb = None
    for l in range(depth):
        g = p["norm_g"][l]
        if l < n_a:
            o, s_new = _hgrn_mixer(h, g[0], p["w_hgrn_in"][l], p["hgrn_lb_logits"],
                                   p["hgrn_onorm_g"][l], hg_state0[l],
                                   layer=l, chunk=hg_chunk, tm=hg_tm)
            states.append(s_new)
            w_o = p["w_hgrn_out"][l]
        else:
            j = l - n_a
            if j == 0:
                k_t = past_k is None and seq % 128 == 0
                tm_kv = min(row_tm, seq)
                k32, v32, kb, vb, qb = _kvq_proj(h, p["kv_norm_g"], g[0], p["w_kv"], p["w_dq"][j],
                                                 tm=tm_kv, k_transposed=k_t)
            else:
                raise NotImplementedError("one attention layer per shared K/V is supported")
            lam_init = 0.8 - 0.6 * math.exp(-0.3 * l)
            if past_k is None:
                o = _attn_prompt(qb, kb, vb, p["diff_lambda"][j], p["diff_subln_g"][j],
                                 lam_init=lam_init, blk=attn_blk)
            else:
                o = _attn_cache(qb, kb, vb, past_k.reshape(bsz, -1, d), past_v.reshape(bsz, -1, d),
                                p["diff_lambda"][j], p["diff_subln_g"][j], lam_init=lam_init)
            w_o = p["w_do"][j]
        h = _out_mlp(o.reshape(n, d), h.reshape(n, d), w_o, g[1], g[2],
                     p["w_up"][l], p["w_down"][l], g[3], tm=row_tm).reshape(bsz, seq, d)
    if k_t:
        k_new = k32.reshape(bsz, heads, 2, DA_D, seq).transpose(0, 4, 1, 2, 3)
    else:
        k_new = k32.reshape(bsz, seq, heads, 2, DA_D)
    v_new = v32.reshape(bsz, seq, heads, HEAD_W)
    return h, k_new, v_new, jnp.stack(states)


def kernel(x_prompt, x_sample, cache_k, cache_v, state_hgrn, norm_g, w_hgrn_in, hgrn_lb_logits,
           hgrn_onorm_g, w_hgrn_out, kv_norm_g, w_kv, w_dq, diff_lambda, diff_subln_g, w_do,
           w_up, w_down):
    p = dict(
        norm_g=norm_g, hgrn_lb_logits=hgrn_lb_logits, hgrn_onorm_g=hgrn_onorm_g,
        kv_norm_g=kv_norm_g, diff_lambda=diff_lambda, diff_subln_g=diff_subln_g,
        w_hgrn_in=w_hgrn_in.astype(BF16), w_hgrn_out=w_hgrn_out.astype(BF16),
        w_kv=w_kv.astype(BF16), w_dq=w_dq.astype(BF16), w_do=w_do.astype(BF16),
        w_up=w_up.astype(BF16), w_down=w_down.astype(BF16),
    )
    n_a = w_hgrn_in.shape[0]
    bp, tp, d = x_prompt.shape
    bs, ts, _ = x_sample.shape
    heads = d // HEAD_W
    s0 = jnp.zeros((n_a, bp, heads, HEAD_W, HEAD_W), x_prompt.dtype)
    y_p, k_p, v_p, st_p = _trunk(x_prompt, s0, None, None, p, hg_chunk=HG_PROMPT_CHUNK,
                                 hg_tm=512, row_tm=512, attn_blk=512)
    y_s, k_s, v_s, st_s = _trunk(x_sample, state_hgrn, cache_k, cache_v, p, hg_chunk=ts,
                                 hg_tm=ts, row_tm=bs * ts, attn_blk=None)
    return (y_p, y_s, k_p, v_p, st_p, k_s, v_s, st_s)
```

```python
import functools
import math

import jax
import jax.numpy as jnp
from jax import lax
from jax.experimental import pallas as pl
from jax.experimental.pallas import tpu as pltpu

F32 = jnp.float32
BF16 = jnp.bfloat16

NORM_EPS = 1e-6
MASK_CHUNK = 64
HEAD_W = 128
DA_D = 64
HG_PROMPT_CHUNK = 64
HG_SUB = 32
EXP_CLAMP = 80.0
NEG_BIG = -1e30
D_FF_TILE = 512
VT_PAD = 16
F32_SUBLANES = 8
LOG2_E = math.log2(math.e)
VMEM_LIMIT = 56 * 1024 * 1024

NT_DIMS = (((1,), (1,)), ((), ()))
TN_DIMS = (((0,), (0,)), ((), ()))


def _rms(x, g):
    ms = jnp.mean(x * x, axis=-1, keepdims=True)
    return x * lax.rsqrt(ms + NORM_EPS) * g


def _sigmoid(x):
    return 1.0 / (1.0 + jnp.exp(-x))


def _const_spec(shape):
    nd = len(shape)
    return pl.BlockSpec(shape, lambda *_: (0,) * nd, pipeline_mode=pl.Buffered(1))


def _hgrn_kernel(h_ref, g_ref, w_ref, lbl_ref, og_ref, s0_ref, o_ref, sout_ref, st_ref,
                 *, layer, chunk, sub, heads):
    t = pl.program_id(1)
    nt = pl.num_programs(1)
    tm = h_ref.shape[1]
    d = h_ref.shape[2]

    @pl.when(t == 0)
    def _():
        for hd in range(heads):
            st_ref[hd] = s0_ref[0, hd].T

    lg = lbl_ref[...]
    lg = jnp.exp(lg - jnp.max(lg, axis=0, keepdims=True))
    lb = jnp.sum(lg[: layer + 1], axis=0, keepdims=True) / jnp.sum(lg, axis=0, keepdims=True)

    a = _rms(h_ref[0], g_ref[...]).astype(BF16)
    proj = jnp.dot(a, w_ref[...], preferred_element_type=F32)
    og = og_ref[...]

    r_i = lax.broadcasted_iota(jnp.int32, (chunk, chunk), 0)
    c_i = lax.broadcasted_iota(jnp.int32, (chunk, chunk), 1)
    tri = (c_i <= r_i).astype(BF16)

    nsub = chunk // sub
    n_chunks = tm // chunk
    heads_sl = [slice(hd * HEAD_W, (hd + 1) * HEAD_W) for hd in range(heads)]

    def prep(c):
        rows = slice(c * chunk, (c + 1) * chunk)
        qr = proj[rows, 0:d]
        fr = proj[rows, d:2 * d]
        vr = proj[rows, 2 * d:3 * d]
        gr = proj[rows, 3 * d:4 * d]

        fg = lb + (1.0 - lb) * _sigmoid(fr)
        gl = jnp.log(fg)
        p0 = gl.astype(BF16)
        r1 = gl - p0.astype(F32)
        p1 = r1.astype(BF16)
        p2 = (r1 - p1.astype(F32)).astype(BF16)
        b = (jnp.dot(tri, p0, preferred_element_type=F32)
             + jnp.dot(tri, p1, preferred_element_type=F32)
             + jnp.dot(tri, p2, preferred_element_type=F32))

        qh = qr * _sigmoid(qr)
        kh = 1.0 - fg
        vb = vr.astype(BF16)
        b_last = b[chunk - 1:chunk, :]
        q_in = (qh * jnp.exp(b)).astype(BF16)
        k_st = (kh * jnp.exp(b_last - b)).astype(BF16)
        e_last = jnp.exp(b_last)

        q_sub, k_sub = [], []
        for j in range(nsub):
            ref = b[j * sub + sub // 2 - 1: j * sub + sub // 2, :]
            ncol = (j + 1) * sub
            q_sub.append((qh[j * sub:ncol] * jnp.exp(jnp.minimum(b[j * sub:ncol] - ref, EXP_CLAMP))
                          ).astype(BF16))
            k_sub.append((kh[:ncol] * jnp.exp(jnp.minimum(ref - b[:ncol], EXP_CLAMP))).astype(BF16))

        gate = gr * _sigmoid(gr)
        return dict(rows=rows, q_in=q_in, k_st=k_st, e_last=e_last, q_sub=q_sub, k_sub=k_sub,
                    vb=vb, gate=gate)

    def state_free_products(pc):
        scores, incs = [], []
        for sl in heads_sl:
            per_sub = []
            for j in range(nsub):
                ncol = (j + 1) * sub
                sc = lax.dot_general(pc["q_sub"][j][:, sl], pc["k_sub"][j][:, sl], NT_DIMS,
                                     preferred_element_type=F32)
                rr = lax.broadcasted_iota(jnp.int32, (sub, ncol), 0) + j * sub
                cc = lax.broadcasted_iota(jnp.int32, (sub, ncol), 1)
                per_sub.append(jnp.where(cc <= rr, sc, 0.0).astype(BF16))
            scores.append(per_sub)
            incs.append(lax.dot_general(pc["vb"][:, sl], pc["k_st"][:, sl], TN_DIMS,
                                        preferred_element_type=F32))
        return scores, incs

    def finish(pc, scores, incs):
        outs = []
        for hd, sl in enumerate(heads_sl):
            o_h = lax.dot_general(pc["q_in"][:, sl], st[hd].astype(BF16), NT_DIMS,
                                  preferred_element_type=F32)
            parts = [jnp.dot(scores[hd][j], pc["vb"][:(j + 1) * sub, sl],
                             preferred_element_type=F32) for j in range(nsub)]
            o_h = o_h + (parts[0] if nsub == 1 else jnp.concatenate(parts, axis=0))
            st[hd] = st[hd] * pc["e_last"][:, sl] + incs[hd]
            outs.append(_rms(o_h, og[:, sl]) * pc["gate"][:, sl])
        o_ref[0, pc["rows"], :] = jnp.concatenate(outs, axis=1).astype(o_ref.dtype)

    st = [st_ref[hd] for hd in range(heads)]
    preps = [prep(0)]
    prev = None
    for c in range(n_chunks):
        if c + 1 < n_chunks:
            preps.append(prep(c + 1))
        pc = preps[c]
        cur = (pc,) + state_free_products(pc)
        if prev is not None:
            finish(*prev)
        prev = cur
    finish(*prev)
    for hd in range(heads):
        st_ref[hd] = st[hd]

    @pl.when(t == nt - 1)
    def _():
        for hd in range(heads):
            sout_ref[0, hd] = st_ref[hd].T


def _hgrn_mixer(h, g, w_in, lb_logits, onorm_g, s0, *, layer, chunk, tm):
    bsz, seq, d = h.shape
    heads = d // HEAD_W
    sub = min(HG_SUB, chunk)
    kern = functools.partial(_hgrn_kernel, layer=layer, chunk=chunk, sub=sub, heads=heads)
    return pl.pallas_call(
        kern,
        grid=(bsz, seq // tm),
        in_specs=[
            pl.BlockSpec((1, tm, d), lambda b, t: (b, t, 0)),
            _const_spec((1, d)),
            _const_spec((d, 4 * d)),
            _const_spec(lb_logits.shape),
            _const_spec((1, d)),
            pl.BlockSpec((1, heads, HEAD_W, HEAD_W), lambda b, t: (b, 0, 0, 0)),
        ],
        out_specs=[
            pl.BlockSpec((1, tm, d), lambda b, t: (b, t, 0)),
            pl.BlockSpec((1, heads, HEAD_W, HEAD_W), lambda b, t: (b, 0, 0, 0)),
        ],
        out_shape=[
            jax.ShapeDtypeStruct((bsz, seq, d), BF16),
            jax.ShapeDtypeStruct((bsz, heads, HEAD_W, HEAD_W), F32),
        ],
        scratch_shapes=[pltpu.VMEM((heads, HEAD_W, HEAD_W), F32)],
        compiler_params=pltpu.CompilerParams(
            dimension_semantics=("parallel", "arbitrary"), vmem_limit_bytes=VMEM_LIMIT),
        name="hgrn_mixer",
    )(h, g.reshape(1, d), w_in, lb_logits, onorm_g.reshape(1, d), s0)


def _out_mlp_kernel(x_ref, h_ref, wo_ref, g1_ref, g2_ref, wu_ref, wd_ref, g3_ref, y_ref):
    m = jnp.dot(x_ref[...], wo_ref[...], preferred_element_type=F32)
    h1 = h_ref[...] + _rms(m, g1_ref[...])
    a2 = _rms(h1, g2_ref[...]).astype(BF16)
    d_ff = wu_ref.shape[1]
    acc = None
    for c in range(d_ff // D_FF_TILE):
        cs = slice(c * D_FF_TILE, (c + 1) * D_FF_TILE)
        u = jnp.dot(a2, wu_ref[:, cs], preferred_element_type=F32)
        u = jnp.maximum(u, 0.0)
        u = (u * u).astype(BF16)
        part = jnp.dot(u, wd_ref[cs, :], preferred_element_type=F32)
        acc = part if acc is None else acc + part
    y_ref[...] = h1 + _rms(acc, g3_ref[...])


def _out_mlp(x, h, w_o, g1, g2, w_up, w_down, g3, *, tm):
    n, d = h.shape
    d_ff = w_up.shape[1]
    row = lambda i: (i, 0)
    return pl.pallas_call(
        _out_mlp_kernel,
        grid=(n // tm,),
        in_specs=[
            pl.BlockSpec((tm, d), row),
            pl.BlockSpec((tm, d), row),
            _const_spec((d, d)),
            _const_spec((1, d)),
            _const_spec((1, d)),
            _const_spec((d, d_ff)),
            _const_spec((d_ff, d)),
            _const_spec((1, d)),
        ],
        out_specs=pl.BlockSpec((tm, d), row),
        out_shape=jax.ShapeDtypeStruct((n, d), F32),
        compiler_params=pltpu.CompilerParams(
            dimension_semantics=("parallel",), vmem_limit_bytes=VMEM_LIMIT),
        name="out_mlp",
    )(x, h, w_o, g1.reshape(1, d), g2.reshape(1, d), w_up, w_down, g3.reshape(1, d))


def _kvq_kernel(h_ref, gkv_ref, gq_ref, wkv_ref, wq_ref, k_ref, v_ref, kb_ref, vb_ref, qb_ref,
                *, k_transposed):
    h = h_ref[0]
    d = h.shape[1]
    akv = _rms(h, gkv_ref[...]).astype(BF16)
    kv = jnp.dot(akv, wkv_ref[...], preferred_element_type=F32)
    k = kv[:, :d]
    v = kv[:, d:]
    k_ref[0] = k.T if k_transposed else k
    v_ref[0] = v
    kb_ref[0] = k.astype(BF16)
    vb_ref[0] = v.astype(BF16)
    aq = _rms(h, gq_ref[...]).astype(BF16)
    q = jnp.dot(aq, wq_ref[...], preferred_element_type=F32)
    qb_ref[0] = (q * (DA_D ** -0.5 * LOG2_E)).astype(BF16)


def _kvq_proj(h, g_kv, g_q, w_kv, w_q, *, tm, k_transposed):
    bsz, seq, d = h.shape
    blk = pl.BlockSpec((1, tm, d), lambda b, t: (b, t, 0))
    if k_transposed:
        k_spec = pl.BlockSpec((1, d, tm), lambda b, t: (b, 0, t))
        k_shape = jax.ShapeDtypeStruct((bsz, d, seq), F32)
    else:
        k_spec, k_shape = blk, jax.ShapeDtypeStruct((bsz, seq, d), F32)
    return pl.pallas_call(
        functools.partial(_kvq_kernel, k_transposed=k_transposed),
        grid=(bsz, seq // tm),
        in_specs=[blk, _const_spec((1, d)), _const_spec((1, d)),
                  _const_spec((d, 2 * d)), _const_spec((d, d))],
        out_specs=[k_spec, blk, blk, blk, blk],
        out_shape=[k_shape, jax.ShapeDtypeStruct((bsz, seq, d), F32),
                   jax.ShapeDtypeStruct((bsz, seq, d), BF16), jax.ShapeDtypeStruct((bsz, seq, d), BF16),
                   jax.ShapeDtypeStruct((bsz, seq, d), BF16)],
        compiler_params=pltpu.CompilerParams(
            dimension_semantics=("parallel", "parallel"), vmem_limit_bytes=VMEM_LIMIT),
        name="kvq_proj",
    )(h, g_kv.reshape(1, d), g_q.reshape(1, d), w_kv, w_q)


def _lambda_value(lp, lam_init):
    a = jnp.sum(lp[0:1] * lp[1:2], axis=-1, keepdims=True)
    b = jnp.sum(lp[2:3] * lp[3:4], axis=-1, keepdims=True)
    return jnp.exp(a) - jnp.exp(b) + lam_init


def _attn_kernel(q_ref, k_ref, v_ref, lam_ref, sg_ref, o_ref,
                 vt_ref, s_ref, smax_ref, m_ref, acc_ref, *, blk, lam_init):
    i = pl.program_id(2)
    seq = k_ref.shape[1]

    @pl.when(i == 0)
    def _():
        def tr(c, carry):
            vb = v_ref[0, pl.ds(pl.multiple_of(c * blk, blk), blk), :].astype(F32)
            vt_ref[c, 0:HEAD_W, :] = vb.T.astype(BF16)
            vt_ref[c, HEAD_W:, :] = jnp.ones((VT_PAD, blk), BF16)
            return carry
        lax.fori_loop(0, seq // blk, tr, 0)

    qt = q_ref[0].astype(F32).T
    row = lax.broadcasted_iota(jnp.int32, qt.shape, 0)
    qc = (jnp.where(row < DA_D, qt, 0.0).astype(BF16), jnp.where(row >= DA_D, qt, 0.0).astype(BF16))

    m_ref[...] = jnp.full(m_ref.shape, NEG_BIG, F32)
    acc_ref[...] = jnp.zeros(acc_ref.shape, F32)

    def scores(j, slot, masked):
        kb = k_ref[0, pl.ds(pl.multiple_of(j * blk, blk), blk), :]
        if masked:
            kr = lax.broadcasted_iota(jnp.int32, (blk, blk), 0) // MASK_CHUNK
            qr = lax.broadcasted_iota(jnp.int32, (blk, blk), 1) // MASK_CHUNK
            vis = kr <= qr
        for c in range(2):
            s = jnp.dot(kb, qc[c], preferred_element_type=F32)
            if masked:
                s = jnp.where(vis, s, NEG_BIG)
            s_ref[slot, c, :, 0:blk] = s
            smax_ref[slot, c, 0:1, :] = jnp.max(s, axis=0, keepdims=True)

    def consume(j, slot):
        vt = vt_ref[j]
        for c in range(2):
            s = s_ref[slot, c, :, 0:blk]
            m_old = m_ref[c, 0:1, :]
            m_new = jnp.maximum(m_old, smax_ref[slot, c, 0:1, :])
            alpha = jnp.exp2(m_old - m_new)
            p = jnp.exp2(s - m_new).astype(BF16)
            m_ref[c, 0:1, :] = m_new
            acc_ref[c] = acc_ref[c] * alpha + jnp.dot(vt, p, preferred_element_type=F32)

    scores(i, 0, True)

    def body(t, carry):
        scores(2 * t, 1, False)
        consume(jnp.where(t == 0, i, 2 * t - 1), 0)
        scores(2 * t + 1, 0, False)
        consume(2 * t, 1)
        return carry
    lax.fori_loop(0, i // 2, body, 0)

    @pl.when(i % 2 == 1)
    def _():
        scores(i - 1, 1, False)
        consume(jnp.where(i == 1, i, i - 2), 0)
        consume(i - 1, 1)

    @pl.when(i % 2 == 0)
    def _():
        consume(jnp.maximum(i - 1, 0), 0)

    lam = _lambda_value(lam_ref[...], lam_init)
    o_t = (acc_ref[0, 0:HEAD_W, :] / acc_ref[0, HEAD_W:HEAD_W + 1, :]
           - lam * (acc_ref[1, 0:HEAD_W, :] / acc_ref[1, HEAD_W:HEAD_W + 1, :]))
    ms = jnp.mean(o_t * o_t, axis=0, keepdims=True)
    o_t = o_t * lax.rsqrt(ms + NORM_EPS)
    o = o_t.T * sg_ref[...] * (1.0 - lam_init)
    o_ref[0] = o.astype(o_ref.dtype)


def _attn_prompt(qb, kb, vb, lam_p, subln_g, *, lam_init, blk):
    bsz, seq, d = qb.shape
    heads = d // HEAD_W
    kern = functools.partial(_attn_kernel, blk=blk, lam_init=lam_init)
    qspec = pl.BlockSpec((1, blk, HEAD_W), lambda b, h, i: (b, i, h))
    kvspec = pl.BlockSpec((1, seq, HEAD_W), lambda b, h, i: (b, 0, h))
    return pl.pallas_call(
        kern,
        grid=(bsz, heads, seq // blk),
        in_specs=[qspec, kvspec, kvspec, pl.BlockSpec(lam_p.shape, lambda b, h, i: (0, 0)),
                  pl.BlockSpec((1, HEAD_W), lambda b, h, i: (0, 0))],
        out_specs=qspec,
        out_shape=jax.ShapeDtypeStruct((bsz, seq, d), BF16),
        scratch_shapes=[
            pltpu.VMEM((seq // blk, HEAD_W + VT_PAD, blk), BF16),
            pltpu.VMEM((2, 2, blk, blk + HEAD_W), F32),
            pltpu.VMEM((2, 2, F32_SUBLANES, blk), F32),
            pltpu.VMEM((2, F32_SUBLANES, blk), F32),
            pltpu.VMEM((2, HEAD_W + VT_PAD, blk), F32),
        ],
        compiler_params=pltpu.CompilerParams(
            dimension_semantics=("parallel", "parallel", "arbitrary"),
            vmem_limit_bytes=VMEM_LIMIT),
        name="diff_attn_prompt",
    )(qb, kb, vb, lam_p, subln_g.reshape(1, HEAD_W))


def _attn_cache_kernel(lam_ref, q_ref, kn_ref, vn_ref, pk_ref, pv_ref, sg_ref, o_ref,
                       *, lam_init, heads):
    tq = q_ref.shape[1]
    past = pk_ref.shape[1]
    lam = _lambda_value(lam_ref[...], lam_init)
    q = q_ref[0]
    lane = lax.broadcasted_iota(jnp.int32, (tq, HEAD_W), 1)
    qpos = (lax.broadcasted_iota(jnp.int32, (tq, past), 0) + past) // MASK_CHUNK
    vis_p = (lax.broadcasted_iota(jnp.int32, (tq, past), 1) // MASK_CHUNK) <= qpos
    qpos_n = (lax.broadcasted_iota(jnp.int32, (tq, tq), 0) + past) // MASK_CHUNK
    vis_n = ((lax.broadcasted_iota(jnp.int32, (tq, tq), 1) + past) // MASK_CHUNK) <= qpos_n
    outs = []
    for hd in range(heads):
        sl = slice(hd * HEAD_W, (hd + 1) * HEAD_W)
        qh = q[:, sl]
        zero = jnp.zeros_like(qh)
        kp = pk_ref[0, :, sl].astype(BF16)
        kn = kn_ref[0, :, sl]
        w_p, w_n = None, None
        for c in range(2):
            qc = jnp.where(lane < DA_D, qh, zero) if c == 0 else jnp.where(lane >= DA_D, qh, zero)
            sp = lax.dot_general(qc, kp, NT_DIMS, preferred_element_type=F32)
            sn = lax.dot_general(qc, kn, NT_DIMS, preferred_element_type=F32)
            sp = jnp.where(vis_p, sp, NEG_BIG)
            sn = jnp.where(vis_n, sn, NEG_BIG)
            m = jnp.maximum(jnp.max(sp, axis=-1, keepdims=True), jnp.max(sn, axis=-1, keepdims=True))
            pp = jnp.exp2(sp - m)
            pn = jnp.exp2(sn - m)
            den = jnp.sum(pp, axis=-1, keepdims=True) + jnp.sum(pn, axis=-1, keepdims=True)
            pp = pp / den
            pn = pn / den
            if c == 0:
                w_p, w_n = pp, pn
            else:
                w_p, w_n = w_p - lam * pp, w_n - lam * pn
        o_h = (jnp.dot(w_p.astype(BF16), pv_ref[0, :, sl].astype(BF16), preferred_element_type=F32)
               + jnp.dot(w_n.astype(BF16), vn_ref[0, :, sl], preferred_element_type=F32))
        outs.append(_rms(o_h, sg_ref[...]) * (1.0 - lam_init))
    o_ref[0] = jnp.concatenate(outs, axis=1).astype(o_ref.dtype)


def _attn_cache(qb, kb, vb, past_k, past_v, lam_p, subln_g, *, lam_init):
    bsz, tq, d = qb.shape
    past = past_k.shape[1]
    heads = d // HEAD_W
    kern = functools.partial(_attn_cache_kernel, lam_init=lam_init, heads=heads)
    new = pl.BlockSpec((1, tq, d), lambda b: (b, 0, 0))
    old = pl.BlockSpec((1, past, d), lambda b: (b, 0, 0))
    return pl.pallas_call(
        kern,
        grid=(bsz,),
        in_specs=[pl.BlockSpec(lam_p.shape, lambda b: (0, 0)), new, new, new, old, old,
                  pl.BlockSpec((1, HEAD_W), lambda b: (0, 0))],
        out_specs=new,
        out_shape=jax.ShapeDtypeStruct((bsz, tq, d), BF16),
        compiler_params=pltpu.CompilerParams(
            dimension_semantics=("parallel",), vmem_limit_bytes=VMEM_LIMIT),
        name="diff_attn_cache",
    )(lam_p, qb, kb, vb, past_k, past_v, subln_g.reshape(1, HEAD_W))


def _trunk(x, hg_state0, past_k, past_v, p, *, hg_chunk, hg_tm, row_tm, attn_blk):
    bsz, seq, d = x.shape
    n = bsz * seq
    depth = p["norm_g"].shape[0]
    n_a = p["w_hgrn_in"].shape[0]
    heads = d // HEAD_W
    h = x
    states = []
    k32 = v32 = kb = vb = None
    for l in range(depth):
        g = p["norm_g"][l]
        if l < n_a:
            o, s_new = _hgrn_mixer(h, g[0], p["w_hgrn_in"][l], p["hgrn_lb_logits"],
                                   p["hgrn_onorm_g"][l], hg_state0[l],
                                   layer=l, chunk=hg_chunk, tm=hg_tm)
            states.append(s_new)
            w_o = p["w_hgrn_out"][l]
        else:
            j = l - n_a
            if j == 0:
                k_t = past_k is None and seq % 128 == 0
                tm_kv = min(row_tm, seq)
                k32, v32, kb, vb, qb = _kvq_proj(h, p["kv_norm_g"], g[0], p["w_kv"], p["w_dq"][j],
                                                 tm=tm_kv, k_transposed=k_t)
            else:
                raise NotImplementedError("one attention layer per shared K/V is supported")
            lam_init = 0.8 - 0.6 * math.exp(-0.3 * l)
            if past_k is None:
                o = _attn_prompt(qb, kb, vb, p["diff_lambda"][j], p["diff_subln_g"][j],
                                 lam_init=lam_init, blk=attn_blk)
            else:
                o = _attn_cache(qb, kb, vb, past_k.reshape(bsz, -1, d), past_v.reshape(bsz, -1, d),
                                p["diff_lambda"][j], p["diff_subln_g"][j], lam_init=lam_init)
            w_o = p["w_do"][j]
        h = _out_mlp(o.reshape(n, d), h.reshape(n, d), w_o, g[1], g[2],
                     p["w_up"][l], p["w_down"][l], g[3], tm=row_tm).reshape(bsz, seq, d)
    if k_t:
        k_new = k32.reshape(bsz, heads, 2, DA_D, seq).transpose(0, 4, 1, 2, 3)
    else:
        k_new = k32.reshape(bsz, seq, heads, 2, DA_D)
    v_new = v32.reshape(bsz, seq, heads, HEAD_W)
    return h, k_new, v_new, jnp.stack(states)


def kernel(x_prompt, x_sample, cache_k, cache_v, state_hgrn, norm_g, w_hgrn_in, hgrn_lb_logits,
           hgrn_onorm_g, w_hgrn_out, kv_norm_g, w_kv, w_dq, diff_lambda, diff_subln_g, w_do,
           w_up, w_down):
    p = dict(
        norm_g=norm_g, hgrn_lb_logits=hgrn_lb_logits, hgrn_onorm_g=hgrn_onorm_g,
        kv_norm_g=kv_norm_g, diff_lambda=diff_lambda, diff_subln_g=diff_subln_g,
        w_hgrn_in=w_hgrn_in.astype(BF16), w_hgrn_out=w_hgrn_out.astype(BF16),
        w_kv=w_kv.astype(BF16), w_dq=w_dq.astype(BF16), w_do=w_do.astype(BF16),
        w_up=w_up.astype(BF16), w_down=w_down.astype(BF16),
    )
    n_a = w_hgrn_in.shape[0]
    bp, tp, d = x_prompt.shape
    bs, ts, _ = x_sample.shape
    heads = d // HEAD_W
    s0 = jnp.zeros((n_a, bp, heads, HEAD_W, HEAD_W), x_prompt.dtype)
    y_p, k_p, v_p, st_p = _trunk(x_prompt, s0, None, None, p, hg_chunk=HG_PROMPT_CHUNK,
                                 hg_tm=512, row_tm=512, attn_blk=512)
    y_s, k_s, v_s, st_s = _trunk(x_sample, state_hgrn, cache_k, cache_v, p, hg_chunk=ts,
                                 hg_tm=ts, row_tm=bs * ts, attn_blk=None)
    return (y_p, y_s, k_p, v_p, st_p, k_s, v_s, st_s)
```

```python
import functools
import math

import jax
import jax.numpy as jnp
from jax import lax
from jax.experimental import pallas as pl
from jax.experimental.pallas import tpu as pltpu

F32 = jnp.float32
BF16 = jnp.bfloat16

NORM_EPS = 1e-6
MASK_CHUNK = 64
HEAD_W = 128
DA_D = 64
HG_PROMPT_CHUNK = 64
HG_SUB = 32
EXP_CLAMP = 80.0
NEG_BIG = -1e30
D_FF_TILE = 512
VT_PAD = 16
F32_SUBLANES = 8
LOG2_E = math.log2(math.e)
VMEM_LIMIT = 56 * 1024 * 1024

NT_DIMS = (((1,), (1,)), ((), ()))
TN_DIMS = (((0,), (0,)), ((), ()))


def _rms(x, g):
    ms = jnp.mean(x * x, axis=-1, keepdims=True)
    return x * lax.rsqrt(ms + NORM_EPS) * g


def _sigmoid(x):
    return 1.0 / (1.0 + jnp.exp(-x))


def _const_spec(shape):
    nd = len(shape)
    return pl.BlockSpec(shape, lambda *_: (0,) * nd, pipeline_mode=pl.Buffered(1))


def _hgrn_kernel(h_ref, g_ref, w_ref, lbl_ref, og_ref, s0_ref, o_ref, sout_ref, st_ref,
                 *, layer, chunk, sub, heads):
    t = pl.program_id(1)
    nt = pl.num_programs(1)
    tm = h_ref.shape[1]
    d = h_ref.shape[2]

    @pl.when(t == 0)
    def _():
        for hd in range(heads):
            st_ref[hd] = s0_ref[0, hd].T

    lg = lbl_ref[...]
    lg = jnp.exp(lg - jnp.max(lg, axis=0, keepdims=True))
    lb = jnp.sum(lg[: layer + 1], axis=0, keepdims=True) / jnp.sum(lg, axis=0, keepdims=True)

    a = _rms(h_ref[0], g_ref[...]).astype(BF16)
    proj = jnp.dot(a, w_ref[...], preferred_element_type=F32)
    og = og_ref[...]

    r_i = lax.broadcasted_iota(jnp.int32, (chunk, chunk), 0)
    c_i = lax.broadcasted_iota(jnp.int32, (chunk, chunk), 1)
    tri = (c_i <= r_i).astype(BF16)

    nsub = chunk // sub
    n_chunks = tm // chunk
    heads_sl = [slice(hd * HEAD_W, (hd + 1) * HEAD_W) for hd in range(heads)]

    def prep(c):
        rows = slice(c * chunk, (c + 1) * chunk)
        qr = proj[rows, 0:d]
        fr = proj[rows, d:2 * d]
        vr = proj[rows, 2 * d:3 * d]
        gr = proj[rows, 3 * d:4 * d]

        fg = lb + (1.0 - lb) * _sigmoid(fr)
        gl = jnp.log(fg)
        p0 = gl.astype(BF16)
        r1 = gl - p0.astype(F32)
        p1 = r1.astype(BF16)
        p2 = (r1 - p1.astype(F32)).astype(BF16)
        b = (jnp.dot(tri, p0, preferred_element_type=F32)
             + jnp.dot(tri, p1, preferred_element_type=F32)
             + jnp.dot(tri, p2, preferred_element_type=F32))

        qh = qr * _sigmoid(qr)
        kh = 1.0 - fg
        vb = vr.astype(BF16)
        b_last = b[chunk - 1:chunk, :]
        q_in = (qh * jnp.exp(b)).astype(BF16)
        k_st = (kh * jnp.exp(b_last - b)).astype(BF16)
        e_last = jnp.exp(b_last)

        q_sub, k_sub = [], []
        for j in range(nsub):
            ref = b[j * sub + sub // 2 - 1: j * sub + sub // 2, :]
            ncol = (j + 1) * sub
            q_sub.append((qh[j * sub:ncol] * jnp.exp(jnp.minimum(b[j * sub:ncol] - ref, EXP_CLAMP))
                          ).astype(BF16))
            k_sub.append((kh[:ncol] * jnp.exp(jnp.minimum(ref - b[:ncol], EXP_CLAMP))).astype(BF16))

        gate = gr * _sigmoid(gr)
        return dict(rows=rows, q_in=q_in, k_st=k_st, e_last=e_last, q_sub=q_sub, k_sub=k_sub,
                    vb=vb, gate=gate)

    def state_free_products(pc):
        scores, incs = [], []
        for sl in heads_sl:
            per_sub = []
            for j in range(nsub):
                ncol = (j + 1) * sub
                sc = lax.dot_general(pc["q_sub"][j][:, sl], pc["k_sub"][j][:, sl], NT_DIMS,
                                     preferred_element_type=F32)
                rr = lax.broadcasted_iota(jnp.int32, (sub, ncol), 0) + j * sub
                cc = lax.broadcasted_iota(jnp.int32, (sub, ncol), 1)
                per_sub.append(jnp.where(cc <= rr, sc, 0.0).astype(BF16))
            scores.append(per_sub)
            incs.append(lax.dot_general(pc["vb"][:, sl], pc["k_st"][:, sl], TN_DIMS,
                                        preferred_element_type=F32))
        return scores, incs

    def finish(pc, scores, incs):
        outs = []
        for hd, sl in enumerate(heads_sl):
            o_h = lax.dot_general(pc["q_in"][:, sl], st[hd].astype(BF16), NT_DIMS,
                                  preferred_element_type=F32)
            parts = [jnp.dot(scores[hd][j], pc["vb"][:(j + 1) * sub, sl],
                             preferred_element_type=F32) for j in range(nsub)]
            o_h = o_h + (parts[0] if nsub == 1 else jnp.concatenate(parts, axis=0))
            st[hd] = st[hd] * pc["e_last"][:, sl] + incs[hd]
            outs.append(_rms(o_h, og[:, sl]) * pc["gate"][:, sl])
        o_ref[0, pc["rows"], :] = jnp.concatenate(outs, axis=1).astype(o_ref.dtype)

    st = [st_ref[hd] for hd in range(heads)]
    preps = [prep(0)]
    prev = None
    for c in range(n_chunks):
        if c + 1 < n_chunks:
            preps.append(prep(c + 1))
        pc = preps[c]
        cur = (pc,) + state_free_products(pc)
        if prev is not None:
            finish(*prev)
        prev = cur
    finish(*prev)
    for hd in range(heads):
        st_ref[hd] = st[hd]

    @pl.when(t == nt - 1)
    def _():
        for hd in range(heads):
            sout_ref[0, hd] = st_ref[hd].T


def _hgrn_mixer(h, g, w_in, lb_logits, onorm_g, s0, *, layer, chunk, tm):
    bsz, seq, d = h.shape
    heads = d // HEAD_W
    sub = min(HG_SUB, chunk)
    kern = functools.partial(_hgrn_kernel, layer=layer, chunk=chunk, sub=sub, heads=heads)
    return pl.pallas_call(
        kern,
        grid=(bsz, seq // tm),
        in_specs=[
            pl.BlockSpec((1, tm, d), lambda b, t: (b, t, 0)),
            _const_spec((1, d)),
            _const_spec((d, 4 * d)),
            _const_spec(lb_logits.shape),
            _const_spec((1, d)),
            pl.BlockSpec((1, heads, HEAD_W, HEAD_W), lambda b, t: (b, 0, 0, 0)),
        ],
        out_specs=[
            pl.BlockSpec((1, tm, d), lambda b, t: (b, t, 0)),
            pl.BlockSpec((1, heads, HEAD_W, HEAD_W), lambda b, t: (b, 0, 0, 0)),
        ],
        out_shape=[
            jax.ShapeDtypeStruct((bsz, seq, d), BF16),
            jax.ShapeDtypeStruct((bsz, heads, HEAD_W, HEAD_W), F32),
        ],
        scratch_shapes=[pltpu.VMEM((heads, HEAD_W, HEAD_W), F32)],
        compiler_params=pltpu.CompilerParams(
            dimension_semantics=("parallel", "arbitrary"), vmem_limit_bytes=VMEM_LIMIT),
        name="hgrn_mixer",
    )(h, g.reshape(1, d), w_in, lb_logits, onorm_g.reshape(1, d), s0)


def _out_mlp_kernel(x_ref, h_ref, wo_ref, g1_ref, g2_ref, wu_ref, wd_ref, g3_ref, y_ref):
    m = jnp.dot(x_ref[...], wo_ref[...], preferred_element_type=F32)
    h1 = h_ref[...] + _rms(m, g1_ref[...])
    a2 = _rms(h1, g2_ref[...]).astype(BF16)
    d_ff = wu_ref.shape[1]
    acc = None
    for c in range(d_ff // D_FF_TILE):
        cs = slice(c * D_FF_TILE, (c + 1) * D_FF_TILE)
        u = jnp.dot(a2, wu_ref[:, cs], preferred_element_type=F32)
        u = jnp.maximum(u, 0.0)
        u = (u * u).astype(BF16)
        part = jnp.dot(u, wd_ref[cs, :], preferred_element_type=F32)
        acc = part if acc is None else acc + part
    y_ref[...] = h1 + _rms(acc, g3_ref[...])


def _out_mlp(x, h, w_o, g1, g2, w_up, w_down, g3, *, tm):
    n, d = h.shape
    d_ff = w_up.shape[1]
    row = lambda i: (i, 0)
    return pl.pallas_call(
        _out_mlp_kernel,
        grid=(n // tm,),
        in_specs=[
            pl.BlockSpec((tm, d), row),
            pl.BlockSpec((tm, d), row),
            _const_spec((d, d)),
            _const_spec((1, d)),
            _const_spec((1, d)),
            _const_spec((d, d_ff)),
            _const_spec((d_ff, d)),
            _const_spec((1, d)),
        ],
        out_specs=pl.BlockSpec((tm, d), row),
        out_shape=jax.ShapeDtypeStruct((n, d), F32),
        compiler_params=pltpu.CompilerParams(
            dimension_semantics=("parallel",), vmem_limit_bytes=VMEM_LIMIT),
        name="out_mlp",
    )(x, h, w_o, g1.reshape(1, d), g2.reshape(1, d), w_up, w_down, g3.reshape(1, d))


def _kvq_kernel(h_ref, gkv_ref, gq_ref, wkv_ref, wq_ref, k_ref, v_ref, kb_ref, vb_ref, qb_ref,
                *, k_transposed):
    h = h_ref[0]
    d = h.shape[1]
    akv = _rms(h, gkv_ref[...]).astype(BF16)
    kv = jnp.dot(akv, wkv_ref[...], preferred_element_type=F32)
    k = kv[:, :d]
    v = kv[:, d:]
    k_ref[0] = k.T if k_transposed else k
    v_ref[0] = v
    kb_ref[0] = k.astype(BF16)
    vb_ref[0] = v.astype(BF16)
    aq = _rms(h, gq_ref[...]).astype(BF16)
    q = jnp.dot(aq, wq_ref[...], preferred_element_type=F32)
    qb_ref[0] = (q * (DA_D ** -0.5 * LOG2_E)).astype(BF16)


def _kvq_proj(h, g_kv, g_q, w_kv, w_q, *, tm, k_transposed):
    bsz, seq, d = h.shape
    blk = pl.BlockSpec((1, tm, d), lambda b, t: (b, t, 0))
    if k_transposed:
        k_spec = pl.BlockSpec((1, d, tm), lambda b, t: (b, 0, t))
        k_shape = jax.ShapeDtypeStruct((bsz, d, seq), F32)
    else:
        k_spec, k_shape = blk, jax.ShapeDtypeStruct((bsz, seq, d), F32)
    return pl.pallas_call(
        functools.partial(_kvq_kernel, k_transposed=k_transposed),
        grid=(bsz, seq // tm),
        in_specs=[blk, _const_spec((1, d)), _const_spec((1, d)),
                  _const_spec((d, 2 * d)), _const_spec((d, d))],
        out_specs=[k_spec, blk, blk, blk, blk],
        out_shape=[k_shape, jax.ShapeDtypeStruct((bsz, seq, d), F32),
                   jax.ShapeDtypeStruct((bsz, seq, d), BF16), jax.ShapeDtypeStruct((bsz, seq, d), BF16),
                   jax.ShapeDtypeStruct((bsz, seq, d), BF16)],
        compiler_params=pltpu.CompilerParams(
            dimension_semantics=("parallel", "parallel"), vmem_limit_bytes=VMEM_LIMIT),
        name="kvq_proj",
    )(h, g_kv.reshape(1, d), g_q.reshape(1, d), w_kv, w_q)


def _lambda_value(lp, lam_init):
    a = jnp.sum(lp[0:1] * lp[1:2], axis=-1, keepdims=True)
    b = jnp.sum(lp[2:3] * lp[3:4], axis=-1, keepdims=True)
    return jnp.exp(a) - jnp.exp(b) + lam_init


def _attn_kernel(q_ref, k_ref, v_ref, lam_ref, sg_ref, o_ref,
                 vt_ref, s_ref, smax_ref, m_ref, acc_ref, *, blk, lam_init):
    i = pl.program_id(2)
    seq = k_ref.shape[1]

    @pl.when(i == 0)
    def _():
        def tr(c, carry):
            vb = v_ref[0, pl.ds(pl.multiple_of(c * blk, blk), blk), :].astype(F32)
            vt_ref[c, 0:HEAD_W, :] = vb.T.astype(BF16)
            vt_ref[c, HEAD_W:, :] = jnp.ones((VT_PAD, blk), BF16)
            return carry
        lax.fori_loop(0, seq // blk, tr, 0)

    qt = q_ref[0].astype(F32).T
    row = lax.broadcasted_iota(jnp.int32, qt.shape, 0)
    qc = (jnp.where(row < DA_D, qt, 0.0).astype(BF16), jnp.where(row >= DA_D, qt, 0.0).astype(BF16))

    m_ref[...] = jnp.full(m_ref.shape, NEG_BIG, F32)
    acc_ref[...] = jnp.zeros(acc_ref.shape, F32)

    def scores(j, slot, masked):
        kb = k_ref[0, pl.ds(pl.multiple_of(j * blk, blk), blk), :]
        if masked:
            kr = lax.broadcasted_iota(jnp.int32, (blk, blk), 0) // MASK_CHUNK
            qr = lax.broadcasted_iota(jnp.int32, (blk, blk), 1) // MASK_CHUNK
            vis = kr <= qr
        for c in range(2):
            s = jnp.dot(kb, qc[c], preferred_element_type=F32)
            if masked:
                s = jnp.where(vis, s, NEG_BIG)
            s_ref[slot, c] = s
            smax_ref[slot, c, 0:1, :] = jnp.max(s, axis=0, keepdims=True)

    def consume(j, slot):
        vt = vt_ref[j]
        for c in range(2):
            s = s_ref[slot, c]
            m_old = m_ref[c, 0:1, :]
            m_new = jnp.maximum(m_old, smax_ref[slot, c, 0:1, :])
            alpha = jnp.exp2(m_old - m_new)
            p = jnp.exp2(s - m_new).astype(BF16)
            m_ref[c, 0:1, :] = m_new
            acc_ref[c] = acc_ref[c] * alpha + jnp.dot(vt, p, preferred_element_type=F32)

    scores(i, 0, True)

    def body(t, carry):
        scores(2 * t, 1, False)
        consume(jnp.where(t == 0, i, 2 * t - 1), 0)
        scores(2 * t + 1, 0, False)
        consume(2 * t, 1)
        return carry
    lax.fori_loop(0, i // 2, body, 0)

    @pl.when(i % 2 == 1)
    def _():
        scores(i - 1, 1, False)
        consume(jnp.where(i == 1, i, i - 2), 0)
        consume(i - 1, 1)

    @pl.when(i % 2 == 0)
    def _():
        consume(jnp.maximum(i - 1, 0), 0)

    lam = _lambda_value(lam_ref[...], lam_init)
    o_t = (acc_ref[0, 0:HEAD_W, :] / acc_ref[0, HEAD_W:HEAD_W + 1, :]
           - lam * (acc_ref[1, 0:HEAD_W, :] / acc_ref[1, HEAD_W:HEAD_W + 1, :]))
    ms = jnp.mean(o_t * o_t, axis=0, keepdims=True)
    o_t = o_t * lax.rsqrt(ms + NORM_EPS)
    o = o_t.T * sg_ref[...] * (1.0 - lam_init)
    o_ref[0] = o.astype(o_ref.dtype)


def _attn_prompt(qb, kb, vb, lam_p, subln_g, *, lam_init, blk):
    bsz, seq, d = qb.shape
    heads = d // HEAD_W
    kern = functools.partial(_attn_kernel, blk=blk, lam_init=lam_init)
    qspec = pl.BlockSpec((1, blk, HEAD_W), lambda b, h, i: (b, i, h))
    kvspec = pl.BlockSpec((1, seq, HEAD_W), lambda b, h, i: (b, 0, h))
    return pl.pallas_call(
        kern,
        grid=(bsz, heads, seq // blk),
        in_specs=[qspec, kvspec, kvspec, pl.BlockSpec(lam_p.shape, lambda b, h, i: (0, 0)),
                  pl.BlockSpec((1, HEAD_W), lambda b, h, i: (0, 0))],
        out_specs=qspec,
        out_shape=jax.ShapeDtypeStruct((bsz, seq, d), BF16),
        scratch_shapes=[
            pltpu.VMEM((seq // blk, HEAD_W + VT_PAD, blk), BF16),
            pltpu.VMEM((2, 2, blk, blk), F32),
            pltpu.VMEM((2, 2, F32_SUBLANES, blk), F32),
            pltpu.VMEM((2, F32_SUBLANES, blk), F32),
            pltpu.VMEM((2, HEAD_W + VT_PAD, blk), F32),
        ],
        compiler_params=pltpu.CompilerParams(
            dimension_semantics=("parallel", "parallel", "arbitrary"),
            vmem_limit_bytes=VMEM_LIMIT),
        name="diff_attn_prompt",
    )(qb, kb, vb, lam_p, subln_g.reshape(1, HEAD_W))


def _attn_cache_kernel(lam_ref, q_ref, kn_ref, vn_ref, pk_ref, pv_ref, sg_ref, o_ref,
                       *, lam_init, heads):
    tq = q_ref.shape[1]
    past = pk_ref.shape[1]
    lam = _lambda_value(lam_ref[...], lam_init)
    q = q_ref[0]
    lane = lax.broadcasted_iota(jnp.int32, (tq, HEAD_W), 1)
    qpos = (lax.broadcasted_iota(jnp.int32, (tq, past), 0) + past) // MASK_CHUNK
    vis_p = (lax.broadcasted_iota(jnp.int32, (tq, past), 1) // MASK_CHUNK) <= qpos
    qpos_n = (lax.broadcasted_iota(jnp.int32, (tq, tq), 0) + past) // MASK_CHUNK
    vis_n = ((lax.broadcasted_iota(jnp.int32, (tq, tq), 1) + past) // MASK_CHUNK) <= qpos_n
    outs = []
    for hd in range(heads):
        sl = slice(hd * HEAD_W, (hd + 1) * HEAD_W)
        qh = q[:, sl]
        zero = jnp.zeros_like(qh)
        kp = pk_ref[0, :, sl].astype(BF16)
        kn = kn_ref[0, :, sl]
        w_p, w_n = None, None
        for c in range(2):
            qc = jnp.where(lane < DA_D, qh, zero) if c == 0 else jnp.where(lane >= DA_D, qh, zero)
            sp = lax.dot_general(qc, kp, NT_DIMS, preferred_element_type=F32)
            sn = lax.dot_general(qc, kn, NT_DIMS, preferred_element_type=F32)
            sp = jnp.where(vis_p, sp, NEG_BIG)
            sn = jnp.where(vis_n, sn, NEG_BIG)
            m = jnp.maximum(jnp.max(sp, axis=-1, keepdims=True), jnp.max(sn, axis=-1, keepdims=True))
            pp = jnp.exp2(sp - m)
            pn = jnp.exp2(sn - m)
            den = jnp.sum(pp, axis=-1, keepdims=True) + jnp.sum(pn, axis=-1, keepdims=True)
            pp = pp / den
            pn = pn / den
            if c == 0:
                w_p, w_n = pp, pn
            else:
                w_p, w_n = w_p - lam * pp, w_n - lam * pn
        o_h = (jnp.dot(w_p.astype(BF16), pv_ref[0, :, sl].astype(BF16), preferred_element_type=F32)
               + jnp.dot(w_n.astype(BF16), vn_ref[0, :, sl], preferred_element_type=F32))
        outs.append(_rms(o_h, sg_ref[...]) * (1.0 - lam_init))
    o_ref[0] = jnp.concatenate(outs, axis=1).astype(o_ref.dtype)


def _attn_cache(qb, kb, vb, past_k, past_v, lam_p, subln_g, *, lam_init):
    bsz, tq, d = qb.shape
    past = past_k.shape[1]
    heads = d // HEAD_W
    kern = functools.partial(_attn_cache_kernel, lam_init=lam_init, heads=heads)
    new = pl.BlockSpec((1, tq, d), lambda b: (b, 0, 0))
    old = pl.BlockSpec((1, past, d), lambda b: (b, 0, 0))
    return pl.pallas_call(
        kern,
        grid=(bsz,),
        in_specs=[pl.BlockSpec(lam_p.shape, lambda b: (0, 0)), new, new, new, old, old,
                  pl.BlockSpec((1, HEAD_W), lambda b: (0, 0))],
        out_specs=new,
        out_shape=jax.ShapeDtypeStruct((bsz, tq, d), BF16),
        compiler_params=pltpu.CompilerParams(
            dimension_semantics=("parallel",), vmem_limit_bytes=VMEM_LIMIT),
        name="diff_attn_cache",
    )(lam_p, qb, kb, vb, past_k, past_v, subln_g.reshape(1, HEAD_W))


def _trunk(x, hg_state0, past_k, past_v, p, *, hg_chunk, hg_tm, row_tm, attn_blk):
    bsz, seq, d = x.shape
    n = bsz * seq
    depth = p["norm_g"].shape[0]
    n_a = p["w_hgrn_in"].shape[0]
    heads = d // HEAD_W
    h = x
    states = []
    k32 = v32 = kb = vb = None
    for l in range(depth):
        g = p["norm_g"][l]
        if l < n_a:
            o, s_new = _hgrn_mixer(h, g[0], p["w_hgrn_in"][l], p["hgrn_lb_logits"],
                                   p["hgrn_onorm_g"][l], hg_state0[l],
                                   layer=l, chunk=hg_chunk, tm=hg_tm)
            states.append(s_new)
            w_o = p["w_hgrn_out"][l]
        else:
            j = l - n_a
            if j == 0:
                k_t = past_k is None and seq % 128 == 0
                tm_kv = min(row_tm, seq)
                k32, v32, kb, vb, qb = _kvq_proj(h, p["kv_norm_g"], g[0], p["w_kv"], p["w_dq"][j],
                                                 tm=tm_kv, k_transposed=k_t)
            else:
                raise NotImplementedError("one attention layer per shared K/V is supported")
            lam_init = 0.8 - 0.6 * math.exp(-0.3 * l)
            if past_k is None:
                o = _attn_prompt(qb, kb, vb, p["diff_lambda"][j], p["diff_subln_g"][j],
                                 lam_init=lam_init, blk=attn_blk)
            else:
                o = _attn_cache(qb, kb, vb, past_k.reshape(bsz, -1, d), past_v.reshape(bsz, -1, d),
                                p["diff_lambda"][j], p["diff_subln_g"][j], lam_init=lam_init)
            w_o = p["w_do"][j]
        h = _out_mlp(o.reshape(n, d), h.reshape(n, d), w_o, g[1], g[2],
                     p["w_up"][l], p["w_down"][l], g[3], tm=row_tm).reshape(bsz, seq, d)
    if k_t:
        k_new = k32.reshape(bsz, heads, 2, DA_D, seq).transpose(0, 4, 1, 2, 3)
    else:
        k_new = k32.reshape(bsz, seq, heads, 2, DA_D)
    v_new = v32.reshape(bsz, seq, heads, HEAD_W)
    return h, k_new, v_new, jnp.stack(states)


def kernel(x_prompt, x_sample, cache_k, cache_v, state_hgrn, norm_g, w_hgrn_in, hgrn_lb_logits,
           hgrn_onorm_g, w_hgrn_out, kv_norm_g, w_kv, w_dq, diff_lambda, diff_subln_g, w_do,
           w_up, w_down):
    p = dict(
        norm_g=norm_g, hgrn_lb_logits=hgrn_lb_logits, hgrn_onorm_g=hgrn_onorm_g,
        kv_norm_g=kv_norm_g, diff_lambda=diff_lambda, diff_subln_g=diff_subln_g,
        w_hgrn_in=w_hgrn_in.astype(BF16), w_hgrn_out=w_hgrn_out.astype(BF16),
        w_kv=w_kv.astype(BF16), w_dq=w_dq.astype(BF16), w_do=w_do.astype(BF16),
        w_up=w_up.astype(BF16), w_down=w_down.astype(BF16),
    )
    n_a = w_hgrn_in.shape[0]
    bp, tp, d = x_prompt.shape
    bs, ts, _ = x_sample.shape
    heads = d // HEAD_W
    s0 = jnp.zeros((n_a, bp, heads, HEAD_W, HEAD_W), x_prompt.dtype)
    y_p, k_p, v_p, st_p = _trunk(x_prompt, s0, None, None, p, hg_chunk=HG_PROMPT_CHUNK,
                                 hg_tm=512, row_tm=1024, attn_blk=512)
    y_s, k_s, v_s, st_s = _trunk(x_sample, state_hgrn, cache_k, cache_v, p, hg_chunk=ts,
                                 hg_tm=ts, row_tm=bs * ts, attn_blk=None)
    return (y_p, y_s, k_p, v_p, st_p, k_s, v_s, st_s)
```
